```python
import math
import jax, jax.numpy as jnp
from jax import lax
import numpy as np

D_MODEL = 2048
BATCH = 16
SEQ = 2048
DEPTH = 1
DEC_BATCH = 16
DEC_SEQ = 16
PAST_LEN = 1024

CHUNK = 64
N_META = 16
N_HEADS = 8
D_HEAD = 64
D_VHEAD = 2 * D_HEAD
QK_W = N_HEADS * D_HEAD
D_ATTN = N_HEADS * D_VHEAD
C_CONV = D_MODEL // 2
CONV_WIDTH = 31
N_BUCKETS = 32
MAX_DISTANCE = 128
Q_BLOCK = 128
EPS = 1e-6
NEG = -1e30
SCALE = D_HEAD ** -0.5
IN_WIDTHS = (QK_W, QK_W, QK_W, QK_W, D_ATTN, D_ATTN, C_CONV, C_CONV, C_CONV, D_MODEL, D_MODEL)
D_IN = sum(IN_WIDTHS)
SPLIT_AT = tuple(int(s) for s in np.cumsum(IN_WIDTHS)[:-1])

kernel_name = 'hybrid_conformer_diffattn_stream_step'


def rmsnorm(x, g):
    x32 = x.astype(jnp.float32)
    y = x32 * lax.rsqrt(jnp.mean(x32 * x32, axis=-1, keepdims=True) + EPS)
    return (y * g.astype(jnp.float32)).astype(x.dtype)


def layernorm(x, g, b):
    x32 = x.astype(jnp.float32)
    mu = jnp.mean(x32, axis=-1, keepdims=True)
    xc = x32 - mu
    y = xc * lax.rsqrt(jnp.mean(xc * xc, axis=-1, keepdims=True) + EPS)
    return (y * g.astype(jnp.float32) + b.astype(jnp.float32)).astype(x.dtype)


def chunk_index(pos):
    return jnp.where(pos < N_META, 0, 1 + (pos - N_META) // CHUNK)


def relative_bias(q_pos, k_pos, table):
    rel = k_pos[None, :] - q_pos[:, None]
    half = N_BUCKETS // 2
    max_exact = half // 2
    n = jnp.abs(rel)
    n_f = jnp.maximum(n, 1).astype(jnp.float32)
    large = max_exact + (jnp.log(n_f / max_exact) / math.log(MAX_DISTANCE / max_exact)
                         * (half - max_exact)).astype(jnp.int32)
    large = jnp.minimum(large, half - 1)
    bucket = jnp.where(rel > 0, half, 0) + jnp.where(n < max_exact, n, large)
    return jnp.transpose(table[bucket], (2, 0, 1)).astype(jnp.float32)


def project_inputs(x, norm_gain, w_in, q_gain, k_gain):
    B, T, _ = x.shape
    u = jnp.einsum('btd,de->bte', rmsnorm(x, norm_gain), w_in)
    q1, q2, k1, k2, v, z_attn, a, g, z_conv, gate_c, gate_a = jnp.split(u, SPLIT_AT, axis=-1)

    def heads(t, gain):
        return rmsnorm(t.reshape(B, T, N_HEADS, D_HEAD), gain)

    q = jnp.concatenate([heads(q1, q_gain), heads(q2, q_gain)], axis=-1).transpose(0, 2, 1, 3)
    k = jnp.concatenate([heads(k1, k_gain), heads(k2, k_gain)], axis=-1).transpose(0, 2, 1, 3)
    v = v.reshape(B, T, N_HEADS, D_VHEAD).transpose(0, 2, 1, 3)
    glu = a * jax.nn.sigmoid(g)
    return q, k, v, z_attn, glu, z_conv, gate_c, gate_a


def diff_attention_block(q, k, v, bias, mask, lam, subln_g, lam_init):
    B, H, Tq, _ = q.shape
    Tk = k.shape[2]
    qm = q.reshape(B, H, Tq, 2, D_HEAD)
    km = k.reshape(B, H, Tk, 2, D_HEAD)
    s = jnp.einsum('bhqmd,bhkmd->bhmqk', qm, km).astype(jnp.float32) * SCALE + bias[None, :, None]
    if mask is not None:
        s = jnp.where(mask, s, NEG)
    p = jax.nn.softmax(s, axis=-1)
    a = p[:, :, 0] - lam * p[:, :, 1]
    o = jnp.einsum('bhqk,bhkd->bhqd', a.astype(v.dtype), v)
    return rmsnorm(o, subln_g) * (1.0 - lam_init)


def prompt_attention(q, k, v, rel_table, lam, subln_g, lam_init):
    B, H, L, _ = q.shape
    n_blk = -(-L // Q_BLOCK)
    L_pad = n_blk * Q_BLOCK
    qb = jnp.pad(q, ((0, 0), (0, 0), (0, L_pad - L), (0, 0)))
    qb = qb.reshape(B, H, n_blk, Q_BLOCK, D_VHEAD).transpose(2, 0, 1, 3, 4)
    k_pos = jnp.arange(L)
    k_chunk = chunk_index(k_pos)

    def one_block(args):
        q_blk, blk = args
        q_pos = blk * Q_BLOCK + jnp.arange(Q_BLOCK)
        bias = relative_bias(q_pos, k_pos, rel_table)
        mask = k_chunk[None, :] <= chunk_index(q_pos)[:, None]
        return diff_attention_block(q_blk, k, v, bias, mask, lam, subln_g, lam_init)

    o = lax.map(one_block, (qb, jnp.arange(n_blk)))
    return o.transpose(1, 2, 0, 3, 4).reshape(B, H, L_pad, D_VHEAD)[:, :, :L]


def causal_depthwise(u_padded, w, b):
    y = lax.conv_general_dilated(u_padded, w[:, None, :].astype(u_padded.dtype), (1,), 'VALID',
                                 dimension_numbers=('NWC', 'WIO', 'NWC'),
                                 feature_group_count=C_CONV)
    return y + b


def finish_layer(x, o_attn, glu_padded, z_attn, z_conv, gate_c, gate_a,
                 conv_w, conv_b, ln_g, ln_b, w_branch_out, w_out):
    B, T, _ = x.shape
    attn = o_attn.transpose(0, 2, 1, 3).reshape(B, T, D_ATTN) * jax.nn.silu(z_attn)
    conv = causal_depthwise(glu_padded, conv_w, conv_b)
    conv = jax.nn.silu(layernorm(conv, ln_g, ln_b)) * jax.nn.silu(z_conv)
    merged = (jax.nn.sigmoid(gate_c) * jnp.einsum('btc,cd->btd', conv, w_branch_out[0])
              + jax.nn.sigmoid(gate_a) * jnp.einsum('btc,cd->btd', attn, w_branch_out[1]))
    return x + jnp.einsum('btd,de->bte', merged, w_out)


def setup_inputs(seed: int = 0) -> dict:
    key = jax.random.key(seed)
    ks = jax.random.split(key, 20)
    f32 = jnp.float32
    nrm = lambda k, shape, s: jax.random.normal(k, shape, f32) * s
    return {
        'x_prompt': nrm(ks[0], (BATCH, SEQ, D_MODEL), 1.0),
        'x_sample': nrm(ks[1], (DEC_BATCH, DEC_SEQ, D_MODEL), 1.0),
        'cache_k': nrm(ks[2], (DEPTH, DEC_BATCH, N_HEADS, PAST_LEN, D_VHEAD), 1.0),
        'cache_v': nrm(ks[3], (DEPTH, DEC_BATCH, N_HEADS, PAST_LEN, D_VHEAD), 1.0),
        'state_conv': nrm(ks[4], (DEPTH, DEC_BATCH, CONV_WIDTH - 1, C_CONV), 0.5),
        'meta_tokens': nrm(ks[5], (N_META, D_MODEL), 1.0),
        'rel_bias': nrm(ks[6], (N_BUCKETS, N_HEADS), 0.5),
        'norm_gain': 1.0 + nrm(ks[7], (DEPTH, D_MODEL), 0.02),
        'w_in': nrm(ks[8], (DEPTH, D_MODEL, D_IN), D_MODEL ** -0.5),
        'q_norm_gain': 1.0 + nrm(ks[9], (DEPTH, N_HEADS, D_HEAD), 0.02),
        'k_norm_gain': 1.0 + nrm(ks[10], (DEPTH, N_HEADS, D_HEAD), 0.02),
        'lambda_qk': nrm(ks[11], (DEPTH, 4, D_HEAD), 0.1),
        'subln_gain': 1.0 + nrm(ks[12], (DEPTH, D_VHEAD), 0.02),
        'conv_w': nrm(ks[13], (DEPTH, CONV_WIDTH, C_CONV), CONV_WIDTH ** -0.5),
        'conv_b': nrm(ks[14], (DEPTH, C_CONV), 0.01),
        'conv_ln_gain': 1.0 + nrm(ks[15], (DEPTH, C_CONV), 0.02),
        'conv_ln_bias': nrm(ks[16], (DEPTH, C_CONV), 0.01),
        'w_branch_out': nrm(ks[17], (DEPTH, 2, C_CONV, D_MODEL), C_CONV ** -0.5),
        'w_out': nrm(ks[18], (DEPTH, D_MODEL, D_MODEL), D_MODEL ** -0.5),
    }


def reference(x_prompt, x_sample, cache_k, cache_v, state_conv, meta_tokens, rel_bias,
              norm_gain, w_in, q_norm_gain, k_norm_gain, lambda_qk, subln_gain,
              conv_w, conv_b, conv_ln_gain, conv_ln_bias, w_branch_out, w_out):
    B = x_prompt.shape[0]
    T_s = x_sample.shape[1]
    past = cache_k.shape[3]
    meta = jnp.broadcast_to(meta_tokens.astype(x_prompt.dtype)[None], (B, N_META, D_MODEL))
    x_p = jnp.concatenate([meta, x_prompt], axis=1)
    x_s = x_sample
    q_pos_s = past + jnp.arange(T_s)
    k_pos_s = jnp.arange(past + T_s)
    k_p_rows, v_p_rows, conv_p_rows, k_s_rows, v_s_rows, conv_s_rows = [], [], [], [], [], []
    for layer in range(DEPTH):
        lam_init = 0.8 - 0.6 * math.exp(-0.3 * layer)
        lq = lambda_qk[layer].astype(jnp.float32)
        lam = jnp.exp(jnp.sum(lq[0] * lq[1])) - jnp.exp(jnp.sum(lq[2] * lq[3])) + lam_init
        q_p, k_p, v_p, za_p, glu_p, zc_p, gc_p, ga_p = project_inputs(
            x_p, norm_gain[layer], w_in[layer], q_norm_gain[layer], k_norm_gain[layer])
        o_p = prompt_attention(q_p, k_p, v_p, rel_bias, lam, subln_gain[layer], lam_init)
        glu_pad_p = jnp.concatenate(
            [jnp.zeros((B, CONV_WIDTH - 1, C_CONV), glu_p.dtype), glu_p], axis=1)
        k_p_rows.append(k_p)
        v_p_rows.append(v_p)
        conv_p_rows.append(glu_pad_p[:, -(CONV_WIDTH - 1):])
        x_p = finish_layer(x_p, o_p, glu_pad_p, za_p, zc_p, gc_p, ga_p, conv_w[layer], conv_b[layer],
                           conv_ln_gain[layer], conv_ln_bias[layer], w_branch_out[layer], w_out[layer])
        q_s, k_s, v_s, za_s, glu_s, zc_s, gc_s, ga_s = project_inputs(
            x_s, norm_gain[layer], w_in[layer], q_norm_gain[layer], k_norm_gain[layer])
        k_all = jnp.concatenate([cache_k[layer].astype(k_s.dtype), k_s], axis=2)
        v_all = jnp.concatenate([cache_v[layer].astype(v_s.dtype), v_s], axis=2)
        bias_s = relative_bias(q_pos_s, k_pos_s, rel_bias)
        o_s = diff_attention_block(q_s, k_all, v_all, bias_s, None, lam, subln_gain[layer], lam_init)
        glu_pad_s = jnp.concatenate([state_conv[layer].astype(glu_s.dtype), glu_s], axis=1)
        k_s_rows.append(k_s)
        v_s_rows.append(v_s)
        conv_s_rows.append(glu_pad_s[:, -(CONV_WIDTH - 1):])
        x_s = finish_layer(x_s, o_s, glu_pad_s, za_s, zc_s, gc_s, ga_s, conv_w[layer], conv_b[layer],
                           conv_ln_gain[layer], conv_ln_bias[layer], w_branch_out[layer], w_out[layer])
    y_prompt = x_p[:, N_META:]
    y_sample = x_s
    k_prompt = jnp.stack(k_p_rows)
    v_prompt = jnp.stack(v_p_rows)
    conv_prompt = jnp.stack(conv_p_rows)
    k_sample = jnp.stack(k_s_rows)
    v_sample = jnp.stack(v_s_rows)
    conv_sample = jnp.stack(conv_s_rows)
    return (y_prompt, y_sample, k_prompt, v_prompt, conv_prompt, k_sample, v_sample, conv_sample)
```

```python
import functools
import math

import jax
import jax.numpy as jnp
import numpy as np
from jax import lax
from jax.experimental import pallas as pl
from jax.experimental.pallas import tpu as pltpu

D_MODEL = 2048
N_HEADS = 8
D_HEAD = 64
D_VHEAD = 2 * D_HEAD
D_ATTN = N_HEADS * D_VHEAD
C_CONV = D_MODEL // 2
CONV_WIDTH = 31
HALO = CONV_WIDTH - 1
N_META = 16
CHUNK = 64
N_BUCKETS = 32
MAX_DISTANCE = 128
EPS = 1e-6
NEG = -1e30
SCALE = D_HEAD ** -0.5

COL_TILE = 1024
STEP_Q, STEP_K, STEP_V, STEP_ZA, STEP_AG0, STEP_AG1, STEP_ZC, STEP_GC0, STEP_GC1, STEP_GA0, STEP_GA1 = range(11)
N_COL_STEPS = 11
HALO_PAD = 32
CONV_ROWS = 32
ATT_TILE = 256
VMEM_LIMIT = 56 * 1024 * 1024

_BF16 = jnp.bfloat16
_F32 = jnp.float32


def _sigmoid(x):
    return 1.0 / (1.0 + jnp.exp(-x))


def _silu(x):
    return x * _sigmoid(x)


def _head_norm(u, gain_ref, scale):
    lane = lax.broadcasted_iota(jnp.int32, (1, D_VHEAD), 1)
    lo = lane < D_HEAD
    outs = []
    for h in range(N_HEADS):
        xh = u[:, h * D_VHEAD:(h + 1) * D_VHEAD]
        sq = xh * xh
        s_lo = jnp.sum(jnp.where(lo, sq, 0.0), axis=-1, keepdims=True)
        s_hi = jnp.sum(jnp.where(lo, 0.0, sq), axis=-1, keepdims=True)
        r = jnp.where(lo, lax.rsqrt(s_lo * (1.0 / D_HEAD) + EPS), lax.rsqrt(s_hi * (1.0 / D_HEAD) + EPS))
        y = xh * r * gain_ref[h:h + 1, :]
        if scale != 1.0:
            y = y * scale
        outs.append(y)
    return outs


def _proj_kernel(x_ref, halo_ref, ng_ref, w_ref, qg_ref, kg_ref, cw_ref, cb_ref, lng_ref, lnb_ref,
                 q_ref, k_ref, v_ref, za_ref, cvb_ref, gc_ref, ga_ref, tail_ref,
                 h_ref, g_ref, zc_ref, *, ns, tt, tiles_per_seq):
    i = pl.program_id(0)
    j = pl.program_id(1)
    seq_rows = HALO_PAD + tt

    @pl.when(j == 0)
    def _():
        x = x_ref[...]
        ms = jnp.mean(x * x, axis=-1, keepdims=True)
        h_ref[...] = (x * lax.rsqrt(ms + EPS) * ng_ref[...]).astype(_BF16)

    u = jnp.dot(h_ref[...], w_ref[...], preferred_element_type=_F32)

    @pl.when(j == STEP_Q)
    def _():
        ys = _head_norm(u, qg_ref, SCALE)
        for h in range(N_HEADS):
            for s in range(ns):
                q_ref[s, h] = ys[h][s * tt:(s + 1) * tt].astype(_BF16)

    @pl.when(j == STEP_K)
    def _():
        ys = _head_norm(u, kg_ref, 1.0)
        for h in range(N_HEADS):
            for s in range(ns):
                k_ref[s, h] = ys[h][s * tt:(s + 1) * tt]

    @pl.when(j == STEP_V)
    def _():
        for h in range(N_HEADS):
            for s in range(ns):
                v_ref[s, h] = u[s * tt:(s + 1) * tt, h * D_VHEAD:(h + 1) * D_VHEAD]

    @pl.when(j == STEP_ZA)
    def _():
        za_ref[...] = _silu(u).astype(_BF16)

    def glu_step(c0):
        half = COL_TILE // 2
        glu = u[:, :half] * _sigmoid(u[:, half:])
        for s in range(ns):
            for c in range(half // 128):
                g_ref[c0 // 128 + c, s * seq_rows + HALO_PAD:(s + 1) * seq_rows, :] = (
                    glu[s * tt:(s + 1) * tt, c * 128:(c + 1) * 128])

    @pl.when(j == STEP_AG0)
    def _():
        if tiles_per_seq == 1:
            for s in range(ns):
                for c in range(C_CONV // 128):
                    g_ref[c, s * seq_rows + 2:s * seq_rows + HALO_PAD, :] = halo_ref[s, :, c * 128:(c + 1) * 128]
        else:
            @pl.when(i % tiles_per_seq == 0)
            def _():
                for c in range(C_CONV // 128):
                    g_ref[c, 2:HALO_PAD, :] = halo_ref[0, :, c * 128:(c + 1) * 128]
        glu_step(0)

    @pl.when(j == STEP_AG1)
    def _():
        glu_step(COL_TILE // 2)

    @pl.when(j == STEP_ZC)
    def _():
        zc_ref[...] = _silu(u)
        rb = min(tt, CONV_ROWS)
        n_lane_tiles = C_CONV // 128

        def conv_block(base, r0):
            accs = []
            for c in range(n_lane_tiles):
                cs = slice(c * 128, (c + 1) * 128)
                acc = jnp.broadcast_to(cb_ref[:, cs], (rb, 128))
                for t in range(CONV_WIDTH):
                    acc = acc + cw_ref[t:t + 1, cs] * g_ref[c, pl.ds(base + r0 + 2 + t, rb), :]
                accs.append(acc)
            tot = accs[0]
            for c in range(1, n_lane_tiles):
                tot = tot + accs[c]
            mu = jnp.sum(tot, axis=-1, keepdims=True) * (1.0 / C_CONV)
            cen = [a - mu for a in accs]
            sq = cen[0] * cen[0]
            for c in range(1, n_lane_tiles):
                sq = sq + cen[c] * cen[c]
            rstd = lax.rsqrt(jnp.sum(sq, axis=-1, keepdims=True) * (1.0 / C_CONV) + EPS)
            return [a * rstd for a in cen]

        def finish_block(base, out_r0, r0):
            ys = conv_block(base, r0)
            for c in range(n_lane_tiles):
                cs = slice(c * 128, (c + 1) * 128)
                y = _silu(ys[c] * lng_ref[:, cs] + lnb_ref[:, cs])
                rows = pl.ds(out_r0 + r0, rb)
                cvb_ref[rows, cs] = (y * zc_ref[rows, cs]).astype(_BF16)

        for s in range(ns):
            base = s * seq_rows
            if tt == rb:
                finish_block(base, s * tt, 0)
            else:
                def body(b, carry, base=base, s=s):
                    finish_block(base, s * tt, pl.multiple_of(b * rb, rb))
                    return carry
                lax.fori_loop(0, tt // rb, body, 0)

        if tiles_per_seq == 1:
            for s in range(ns):
                for c in range(n_lane_tiles):
                    tail_ref[s, :, c * 128:(c + 1) * 128] = g_ref[c, (s + 1) * seq_rows - HALO:(s + 1) * seq_rows, :]
        else:
            @pl.when(i % tiles_per_seq == tiles_per_seq - 1)
            def _():
                for c in range(n_lane_tiles):
                    tail_ref[0, :, c * 128:(c + 1) * 128] = g_ref[c, seq_rows - HALO:seq_rows, :]
            g_ref[:, 0:HALO_PAD, :] = g_ref[:, tt:tt + HALO_PAD, :]

    @pl.when((j == STEP_GC0) | (j == STEP_GC1))
    def _():
        gc_ref[...] = _sigmoid(u).astype(_BF16)

    @pl.when((j == STEP_GA0) | (j == STEP_GA1))
    def _():
        ga_ref[...] = _sigmoid(u).astype(_BF16)


def _project(x, halo, ng, w, qg, kg, cw, cb, lng, lnb, *, n_seq, seq_len, tm):
    rows = n_seq * seq_len
    if seq_len >= tm:
        assert seq_len % tm == 0
        ns, tt, tps = 1, tm, seq_len // tm
    else:
        assert tm % seq_len == 0 and rows % tm == 0
        ns, tt, tps = tm // seq_len, seq_len, 1
    n_tiles = rows // tm
    halo_bcast = halo.shape[0] == 1

    def seq_map(i, j):
        return (i // tps, 0, i % tps, 0) if tps > 1 else (i, 0, 0, 0)

    def halo_map(i, j):
        if halo_bcast:
            return (0, 0, 0)
        return (i // tps, 0, 0) if tps > 1 else (i, 0, 0)

    def tail_map(i, j):
        return (i // tps, 0, 0) if tps > 1 else (i, 0, 0)

    const2 = lambda i, j: (0, 0)
    head_blk = (ns, N_HEADS, tt, D_VHEAD)
    grid = (n_tiles, N_COL_STEPS)
    kern = functools.partial(_proj_kernel, ns=ns, tt=tt, tiles_per_seq=tps)
    out_shape = (
        jax.ShapeDtypeStruct((n_seq, N_HEADS, seq_len, D_VHEAD), _BF16),
        jax.ShapeDtypeStruct((n_seq, N_HEADS, seq_len, D_VHEAD), _F32),
        jax.ShapeDtypeStruct((n_seq, N_HEADS, seq_len, D_VHEAD), _F32),
        jax.ShapeDtypeStruct((rows, D_ATTN), _BF16),
        jax.ShapeDtypeStruct((rows, C_CONV), _BF16),
        jax.ShapeDtypeStruct((rows, D_MODEL), _BF16),
        jax.ShapeDtypeStruct((rows, D_MODEL), _BF16),
        jax.ShapeDtypeStruct((n_seq, HALO, C_CONV), _F32),
    )
    in_specs = [
        pl.BlockSpec((tm, D_MODEL), lambda i, j: (i, 0)),
        pl.BlockSpec((1 if halo_bcast or tps > 1 else ns, HALO, C_CONV), halo_map),
        pl.BlockSpec((1, D_MODEL), const2),
        pl.BlockSpec((D_MODEL, COL_TILE), lambda i, j: (0, j)),
        pl.BlockSpec((N_HEADS, D_VHEAD), const2),
        pl.BlockSpec((N_HEADS, D_VHEAD), const2),
        pl.BlockSpec((CONV_WIDTH, C_CONV), const2),
        pl.BlockSpec((1, C_CONV), const2),
        pl.BlockSpec((1, C_CONV), const2),
        pl.BlockSpec((1, C_CONV), const2),
    ]
    out_specs = (
        pl.BlockSpec(head_blk, seq_map),
        pl.BlockSpec(head_blk, seq_map),
        pl.BlockSpec(head_blk, seq_map),
        pl.BlockSpec((tm, D_ATTN), lambda i, j: (i, 0)),
        pl.BlockSpec((tm, C_CONV), lambda i, j: (i, 0)),
        pl.BlockSpec((tm, COL_TILE), lambda i, j: (i, jnp.clip(j - STEP_GC0, 0, 1))),
        pl.BlockSpec((tm, COL_TILE), lambda i, j: (i, jnp.clip(j - STEP_GA0, 0, 1))),
        pl.BlockSpec((1 if tps > 1 else ns, HALO, C_CONV), tail_map),
    )
    scratch = [
        pltpu.VMEM((tm, D_MODEL), _BF16),
        pltpu.VMEM((C_CONV // 128, ns * (HALO_PAD + tt), 128), _F32),
        pltpu.VMEM((tm, C_CONV), _F32),
    ]
    return pl.pallas_call(
        kern, grid=grid, in_specs=in_specs, out_specs=out_specs, out_shape=out_shape,
        scratch_shapes=scratch,
        compiler_params=pltpu.CompilerParams(
            dimension_semantics=("arbitrary", "arbitrary"), vmem_limit_bytes=VMEM_LIMIT),
        name="proj",
    )(x, halo, ng, w, qg, kg, cw, cb, lng, lnb)


def _attn_kernel(lam_ref, q_ref, k_ref, v_ref, km_ref, vm_ref, za_ref, bd0_ref, bd1_ref, bm_ref, g_ref,
                 o_ref, kb_ref, vt_ref, m_ref, l_ref, acc_ref, *, seq_len):
    T = ATT_TILE
    n_q = seq_len // T
    lam = lam_ref[0]
    kb_ref[...] = k_ref[0, 0].astype(_BF16)
    vt_ref[...] = v_ref[0, 0].T.astype(_BF16)
    kmb = km_ref[0].astype(_BF16)
    vmt = vm_ref[0].T.astype(_BF16)
    lane = lax.broadcasted_iota(jnp.int32, (1, D_VHEAD), 1)
    lo = lane < D_HEAD
    nt = (((1,), (1,)), ((), ()))

    def q_tile(i, carry):
        r0 = pl.multiple_of(i * T, T)
        q = q_ref[0, 0, pl.ds(r0, T), :]
        zero = jnp.zeros_like(q)
        qs = (jnp.where(lo, q, zero), jnp.where(lo, zero, q))

        def tile(kt, vtt, bias, first):
            for mp in range(2):
                s = lax.dot_general(kt, qs[mp], nt, preferred_element_type=_F32)
                if bias is not None:
                    s = s + bias
                tmax = jnp.max(s, axis=0, keepdims=True)
                if first:
                    m_new = tmax
                else:
                    m_old = m_ref[mp]
                    m_new = jnp.maximum(m_old, tmax)
                    alpha = jnp.exp(m_old - m_new)
                p = jnp.exp(s - m_new)
                psum = jnp.sum(p, axis=0, keepdims=True)
                pv = jnp.dot(vtt, p.astype(_BF16), preferred_element_type=_F32)
                if first:
                    l_ref[mp] = psum
                    acc_ref[mp] = pv
                else:
                    l_ref[mp] = alpha * l_ref[mp] + psum
                    acc_ref[mp] = alpha * acc_ref[mp] + pv
                m_ref[mp] = m_new

        tile(kb_ref[pl.ds(r0, T), :], vt_ref[:, pl.ds(r0, T)], bd0_ref[0], True)

        @pl.when(i >= 1)
        def _():
            r1 = pl.multiple_of((i - 1) * T, T)
            tile(kb_ref[pl.ds(r1, T), :], vt_ref[:, pl.ds(r1, T)], bd1_ref[0], False)

        def far(jk, c):
            rj = pl.multiple_of(jk * T, T)
            tile(kb_ref[pl.ds(rj, T), :], vt_ref[:, pl.ds(rj, T)], None, False)
            return c
        lax.fori_loop(0, jnp.maximum(i - 1, 0), far, 0)

        @pl.when(i == 0)
        def _():
            tile(kmb, vmt, bm_ref[0], False)

        @pl.when(i > 0)
        def _():
            tile(kmb, vmt, None, False)

        o = acc_ref[0] / l_ref[0] - lam * (acc_ref[1] / l_ref[1])
        ms = jnp.mean(o * o, axis=0, keepdims=True)
        y = o * lax.rsqrt(ms + EPS) * g_ref[...]
        yt = y.T
        o_ref[0, pl.ds(r0, T), :] = (yt * za_ref[0, pl.ds(r0, T), :].astype(_F32)).astype(_BF16)
        return carry

    lax.fori_loop(0, n_q, q_tile, 0)


def _attention(lam, q, k, v, km, vm, za, bd0, bd1, bm, g):
    n_b, _, seq_len, _ = q.shape
    T = ATT_TILE
    assert seq_len % T == 0
    kern = functools.partial(_attn_kernel, seq_len=seq_len)
    bh = lambda b, h: (b, h, 0, 0)
    hd = lambda b, h: (h, 0, 0)
    in_specs = [
        pl.BlockSpec(memory_space=pltpu.SMEM),
        pl.BlockSpec((1, 1, seq_len, D_VHEAD), bh),
        pl.BlockSpec((1, 1, seq_len, D_VHEAD), bh),
        pl.BlockSpec((1, 1, seq_len, D_VHEAD), bh),
        pl.BlockSpec((1, N_META, D_VHEAD), hd),
        pl.BlockSpec((1, N_META, D_VHEAD), hd),
        pl.BlockSpec((1, seq_len, D_VHEAD), lambda b, h: (b, 0, h)),
        pl.BlockSpec((1, T, T), hd),
        pl.BlockSpec((1, T, T), hd),
        pl.BlockSpec((1, N_META, T), hd),
        pl.BlockSpec((D_VHEAD, T), lambda b, h: (0, 0)),
    ]
    return pl.pallas_call(
        kern, grid=(n_b, N_HEADS), in_specs=in_specs,
        out_specs=pl.BlockSpec((1, seq_len, D_VHEAD), lambda b, h: (b, 0, h)),
        out_shape=jax.ShapeDtypeStruct((n_b, seq_len, D_ATTN), _BF16),
        scratch_shapes=[
            pltpu.VMEM((seq_len, D_VHEAD), _BF16),
            pltpu.VMEM((D_VHEAD, seq_len), _BF16),
            pltpu.VMEM((2, 1, T), _F32),
            pltpu.VMEM((2, 1, T), _F32),
            pltpu.VMEM((2, D_VHEAD, T), _F32),
        ],
        compiler_params=pltpu.CompilerParams(
            dimension_semantics=("arbitrary", "arbitrary"), vmem_limit_bytes=VMEM_LIMIT),
        name="attn",
    )(lam, q, k, v, km, vm, za, bd0, bd1, bm, g)


def _decode_kernel(lam_ref, q_ref, kn_ref, vn_ref, ck_ref, cv_ref, za_ref, bc_ref, bn_ref, g_ref, o_ref):
    lam = lam_ref[0]
    lane = lax.broadcasted_iota(jnp.int32, (1, D_VHEAD), 1)
    lo = lane < D_HEAD
    nt = (((1,), (1,)), ((), ()))
    for h in range(N_HEADS):
        q = q_ref[0, h]
        zero = jnp.zeros_like(q)
        qs = (jnp.where(lo, q, zero), jnp.where(lo, zero, q))
        ck = ck_ref[0, 0, h].astype(_BF16)
        kn = kn_ref[0, h].astype(_BF16)
        ps = []
        for mp in range(2):
            sc = lax.dot_general(qs[mp], ck, nt, preferred_element_type=_F32) + bc_ref[h]
            sn = lax.dot_general(qs[mp], kn, nt, preferred_element_type=_F32) + bn_ref[h]
            m = jnp.maximum(jnp.max(sc, axis=-1, keepdims=True), jnp.max(sn, axis=-1, keepdims=True))
            ec = jnp.exp(sc - m)
            en = jnp.exp(sn - m)
            l = jnp.sum(ec, axis=-1, keepdims=True) + jnp.sum(en, axis=-1, keepdims=True)
            ps.append((ec / l, en / l))
        ac = (ps[0][0] - lam * ps[1][0]).astype(_BF16)
        an = (ps[0][1] - lam * ps[1][1]).astype(_BF16)
        o = (jnp.dot(ac, cv_ref[0, 0, h].astype(_BF16), preferred_element_type=_F32)
             + jnp.dot(an, vn_ref[0, h].astype(_BF16), preferred_element_type=_F32))
        ms = jnp.mean(o * o, axis=-1, keepdims=True)
        y = o * lax.rsqrt(ms + EPS) * g_ref[...]
        cs = slice(h * D_VHEAD, (h + 1) * D_VHEAD)
        o_ref[0, :, cs] = (y * za_ref[0, :, cs].astype(_F32)).astype(_BF16)


def _decode_attention(lam, q, kn, vn, ck, cv, za, bc, bn, g):
    n_b, _, t_s, _ = q.shape
    past = ck.shape[3]
    b4 = lambda b: (b, 0, 0, 0)
    c3 = lambda b: (0, 0, 0)
    in_specs = [
        pl.BlockSpec(memory_space=pltpu.SMEM),
        pl.BlockSpec((1, N_HEADS, t_s, D_VHEAD), b4),
        pl.BlockSpec((1, N_HEADS, t_s, D_VHEAD), b4),
        pl.BlockSpec((1, N_HEADS, t_s, D_VHEAD), b4),
        pl.BlockSpec((1, 1, N_HEADS, past, D_VHEAD), lambda b: (0, b, 0, 0, 0)),
        pl.BlockSpec((1, 1, N_HEADS, past, D_VHEAD), lambda b: (0, b, 0, 0, 0)),
        pl.BlockSpec((1, t_s, D_ATTN), lambda b: (b, 0, 0)),
        pl.BlockSpec((N_HEADS, t_s, past), c3),
        pl.BlockSpec((N_HEADS, t_s, t_s), c3),
        pl.BlockSpec((1, D_VHEAD), lambda b: (0, 0)),
    ]
    return pl.pallas_call(
        _decode_kernel, grid=(n_b,), in_specs=in_specs,
        out_specs=pl.BlockSpec((1, t_s, D_ATTN), lambda b: (b, 0, 0)),
        out_shape=jax.ShapeDtypeStruct((n_b, t_s, D_ATTN), _BF16),
        compiler_params=pltpu.CompilerParams(
            dimension_semantics=("arbitrary",), vmem_limit_bytes=VMEM_LIMIT),
        name="decode_attn",
    )(lam, q, kn, vn, ck, cv, za, bc, bn, g)


def _finish_kernel(x_ref, at_ref, cv_ref, gc_ref, ga_ref, wb0_ref, wb1_ref, wo_ref, y_ref):
    pc = jnp.dot(cv_ref[...], wb0_ref[...], preferred_element_type=_F32)
    pa = jnp.dot(at_ref[...], wb1_ref[...], preferred_element_type=_F32)
    merged = gc_ref[...].astype(_F32) * pc + ga_ref[...].astype(_F32) * pa
    y_ref[...] = x_ref[...] + jnp.dot(merged.astype(_BF16), wo_ref[...], preferred_element_type=_F32)


def _finish(x, at, cv, gc, ga, wb0, wb1, wo, *, tm):
    rows = x.shape[0]
    assert rows % tm == 0
    row = lambda i: (i, 0)
    const = lambda i: (0, 0)
    single = pl.Buffered(1)
    in_specs = [
        pl.BlockSpec((tm, D_MODEL), row),
        pl.BlockSpec((tm, D_ATTN), row),
        pl.BlockSpec((tm, C_CONV), row),
        pl.BlockSpec((tm, D_MODEL), row),
        pl.BlockSpec((tm, D_MODEL), row),
        pl.BlockSpec((C_CONV, D_MODEL), const, pipeline_mode=single),
        pl.BlockSpec((D_ATTN, D_MODEL), const, pipeline_mode=single),
        pl.BlockSpec((D_MODEL, D_MODEL), const, pipeline_mode=single),
    ]
    return pl.pallas_call(
        _finish_kernel, grid=(rows // tm,), in_specs=in_specs,
        out_specs=pl.BlockSpec((tm, D_MODEL), row),
        out_shape=jax.ShapeDtypeStruct((rows, D_MODEL), _F32),
        compiler_params=pltpu.CompilerParams(
            dimension_semantics=("arbitrary",), vmem_limit_bytes=VMEM_LIMIT),
        name="finish",
    )(x, at, cv, gc, ga, wb0, wb1, wo)


def _bucket_bias(rel, table):
    half = N_BUCKETS // 2
    max_exact = half // 2
    n = jnp.abs(rel)
    n_f = jnp.maximum(n, 1).astype(jnp.float32)
    large = max_exact + (jnp.log(n_f / max_exact) / math.log(MAX_DISTANCE / max_exact)
                         * (half - max_exact)).astype(jnp.int32)
    large = jnp.minimum(large, half - 1)
    bucket = jnp.where(rel > 0, half, 0) + jnp.where(n < max_exact, n, large)
    return table[bucket].astype(jnp.float32)


def _permute_w_in(w):
    qk_w = N_HEADS * D_HEAD
    d = w.shape[0]

    def heads(c0):
        return w[:, c0:c0 + 2 * qk_w].reshape(d, 2, N_HEADS, D_HEAD).transpose(0, 2, 1, 3).reshape(d, 2 * qk_w)

    c_ag = 4 * qk_w + 2 * D_ATTN
    half = COL_TILE // 2
    ag = w[:, c_ag:c_ag + 2 * C_CONV].reshape(d, 2, C_CONV // half, half).transpose(0, 2, 1, 3).reshape(d, 2 * C_CONV)
    parts = [heads(0), heads(2 * qk_w), w[:, 4 * qk_w:c_ag], ag, w[:, c_ag + 2 * C_CONV:]]
    return jnp.concatenate(parts, axis=1).astype(_BF16)


def kernel(x_prompt, x_sample, cache_k, cache_v, state_conv, meta_tokens, rel_bias, norm_gain, w_in,
           q_norm_gain, k_norm_gain, lambda_qk, subln_gain, conv_w, conv_b, conv_ln_gain, conv_ln_bias,
           w_branch_out, w_out):
    n_b, seq, _ = x_prompt.shape
    n_s, t_s, _ = x_sample.shape
    past = cache_k.shape[3]
    assert w_in.shape[0] == 1 and t_s == N_META and seq % ATT_TILE == 0
    layer = 0
    lam_init = 0.8 - 0.6 * math.exp(-0.3 * layer)

    lq = lambda_qk[layer].astype(_F32)
    lam = (jnp.exp(jnp.sum(lq[0] * lq[1])) - jnp.exp(jnp.sum(lq[2] * lq[3])) + lam_init).reshape(1)

    w = _permute_w_in(w_in[layer])
    ng = norm_gain[layer].reshape(1, D_MODEL)
    qg = jnp.tile(q_norm_gain[layer], (1, 2))
    kg = jnp.tile(k_norm_gain[layer], (1, 2))
    cw = conv_w[layer]
    cb = conv_b[layer].reshape(1, C_CONV)
    lng = conv_ln_gain[layer].reshape(1, C_CONV)
    lnb = conv_ln_bias[layer].reshape(1, C_CONV)
    wb0 = w_branch_out[layer, 0].astype(_BF16)
    wb1 = w_branch_out[layer, 1].astype(_BF16)
    wo = w_out[layer].astype(_BF16)
    g_out = subln_gain[layer].astype(_F32) * (1.0 - lam_init)
    proj = functools.partial(_project, ng=ng, w=w, qg=qg, kg=kg, cw=cw, cb=cb, lng=lng, lnb=lnb)

    x_small = jnp.concatenate([meta_tokens.astype(_F32), x_sample.reshape(n_s * t_s, D_MODEL)], axis=0)
    halo_small = jnp.concatenate([jnp.zeros((1, HALO, C_CONV), _F32), state_conv[layer].astype(_F32)], axis=0)
    q_sm, k_sm, v_sm, za_sm, cvb_sm, gc_sm, ga_sm, tail_sm = proj(
        x_small, halo_small, n_seq=n_s + 1, seq_len=t_s, tm=(n_s + 1) * t_s)

    q_p, k_p, v_p, za_p, cvb_p, gc_p, ga_p, tail_p = proj(
        x_prompt.reshape(n_b * seq, D_MODEL), tail_sm[0:1], n_seq=n_b, seq_len=seq, tm=512)

    T = ATT_TILE
    far = rel_bias[N_BUCKETS // 2 - 1].astype(_F32)
    r = jnp.arange(T)
    rel_d = r[:, None] - r[None, :]
    allowed = (r[:, None] // CHUNK) <= (r[None, :] // CHUNK)
    bd0 = jnp.where(allowed[None], jnp.transpose(_bucket_bias(rel_d, rel_bias) - far, (2, 0, 1)), NEG)
    bd1 = jnp.transpose(_bucket_bias(rel_d - T, rel_bias) - far, (2, 0, 1))
    rel_m = jnp.arange(N_META)[:, None] - (N_META + r[None, :])
    bm = jnp.transpose(_bucket_bias(rel_m, rel_bias) - far, (2, 0, 1))
    g_t = jnp.broadcast_to(g_out[:, None], (D_VHEAD, T))

    at_p = _attention(lam, q_p, k_p, v_p, k_sm[0], v_sm[0], za_p.reshape(n_b, seq, D_ATTN), bd0, bd1, bm, g_t)

    q_pos = past + jnp.arange(t_s)
    bias_c = jnp.transpose(_bucket_bias(jnp.arange(past)[None, :] - q_pos[:, None], rel_bias), (2, 0, 1))
    bias_n = jnp.transpose(_bucket_bias(q_pos[None, :] - q_pos[:, None], rel_bias), (2, 0, 1))
    at_s = _decode_attention(lam, q_sm[1:], k_sm[1:], v_sm[1:], cache_k, cache_v,
                             za_sm[t_s:].reshape(n_s, t_s, D_ATTN), bias_c, bias_n, g_out.reshape(1, D_VHEAD))

    y_p = _finish(x_prompt.reshape(n_b * seq, D_MODEL), at_p.reshape(n_b * seq, D_ATTN), cvb_p, gc_p, ga_p,
                  wb0, wb1, wo, tm=512)
    y_s = _finish(x_sample.reshape(n_s * t_s, D_MODEL), at_s.reshape(n_s * t_s, D_ATTN), cvb_sm[t_s:],
                  gc_sm[t_s:], ga_sm[t_s:], wb0, wb1, wo, tm=n_s * t_s)

    k_meta = jnp.broadcast_to(k_sm[0][None], (n_b, N_HEADS, N_META, D_VHEAD))
    v_meta = jnp.broadcast_to(v_sm[0][None], (n_b, N_HEADS, N_META, D_VHEAD))
    k_prompt = jnp.concatenate([k_meta, k_p], axis=2)[None]
    v_prompt = jnp.concatenate([v_meta, v_p], axis=2)[None]
    return (y_p.reshape(n_b, seq, D_MODEL), y_s.reshape(n_s, t_s, D_MODEL), k_prompt, v_prompt,
            tail_p[None], k_sm[1:][None], v_sm[1:][None], tail_sm[1:][None])
```

```python
import functools
import math

import jax
import jax.numpy as jnp
import numpy as np
from jax import lax
from jax.experimental import pallas as pl
from jax.experimental.pallas import tpu as pltpu

D_MODEL = 2048
N_HEADS = 8
D_HEAD = 64
D_VHEAD = 2 * D_HEAD
D_ATTN = N_HEADS * D_VHEAD
C_CONV = D_MODEL // 2
CONV_WIDTH = 31
HALO = CONV_WIDTH - 1
N_META = 16
CHUNK = 64
N_BUCKETS = 32
MAX_DISTANCE = 128
EPS = 1e-6
NEG = -1e30
SCALE = D_HEAD ** -0.5

COL_TILE = 1024
STEP_Q, STEP_K, STEP_V, STEP_ZA, STEP_AG0, STEP_AG1, STEP_ZC, STEP_GC0, STEP_GC1, STEP_GA0, STEP_GA1 = range(11)
N_COL_STEPS = 11
HALO_PAD = 32
CONV_ROWS = 32
ATT_TILE = 256
VMEM_LIMIT = 56 * 1024 * 1024

_BF16 = jnp.bfloat16
_F32 = jnp.float32


def _sigmoid(x):
    return 1.0 / (1.0 + jnp.exp(-x))


def _silu(x):
    return x * _sigmoid(x)


def _head_norm(u, gain_ref, scale):
    lane = lax.broadcasted_iota(jnp.int32, (1, D_VHEAD), 1)
    lo = lane < D_HEAD
    outs = []
    for h in range(N_HEADS):
        xh = u[:, h * D_VHEAD:(h + 1) * D_VHEAD]
        sq = xh * xh
        s_lo = jnp.sum(jnp.where(lo, sq, 0.0), axis=-1, keepdims=True)
        s_hi = jnp.sum(jnp.where(lo, 0.0, sq), axis=-1, keepdims=True)
        r = jnp.where(lo, lax.rsqrt(s_lo * (1.0 / D_HEAD) + EPS), lax.rsqrt(s_hi * (1.0 / D_HEAD) + EPS))
        y = xh * r * gain_ref[h:h + 1, :]
        if scale != 1.0:
            y = y * scale
        outs.append(y)
    return outs


def _proj_kernel(x_ref, halo_ref, ng_ref, w_ref, qg_ref, kg_ref, cw_ref, cb_ref, lng_ref, lnb_ref,
                 q_ref, k_ref, v_ref, za_ref, cvb_ref, gc_ref, ga_ref, tail_ref,
                 h_ref, g_ref, zc_ref, *, ns, tt, tiles_per_seq):
    i = pl.program_id(0)
    j = pl.program_id(1)
    seq_rows = HALO_PAD + tt

    @pl.when(j == 0)
    def _():
        x = x_ref[...]
        ms = jnp.mean(x * x, axis=-1, keepdims=True)
        h_ref[...] = (x * lax.rsqrt(ms + EPS) * ng_ref[...]).astype(_BF16)

    u = jnp.dot(h_ref[...], w_ref[...], preferred_element_type=_F32)

    @pl.when(j == STEP_Q)
    def _():
        ys = _head_norm(u, qg_ref, SCALE)
        for h in range(N_HEADS):
            for s in range(ns):
                q_ref[s, h] = ys[h][s * tt:(s + 1) * tt].astype(_BF16)

    @pl.when(j == STEP_K)
    def _():
        ys = _head_norm(u, kg_ref, 1.0)
        for h in range(N_HEADS):
            for s in range(ns):
                k_ref[s, h] = ys[h][s * tt:(s + 1) * tt]

    @pl.when(j == STEP_V)
    def _():
        for h in range(N_HEADS):
            for s in range(ns):
                v_ref[s, h] = u[s * tt:(s + 1) * tt, h * D_VHEAD:(h + 1) * D_VHEAD]

    @pl.when(j == STEP_ZA)
    def _():
        za_ref[...] = _silu(u).astype(_BF16)

    def glu_step(c0):
        half = COL_TILE // 2
        glu = u[:, :half] * _sigmoid(u[:, half:])
        for s in range(ns):
            for c in range(half // 128):
                g_ref[c0 // 128 + c, s * seq_rows + HALO_PAD:(s + 1) * seq_rows, :] = (
                    glu[s * tt:(s + 1) * tt, c * 128:(c + 1) * 128])

    @pl.when(j == STEP_AG0)
    def _():
        if tiles_per_seq == 1:
            for s in range(ns):
                for c in range(C_CONV // 128):
                    g_ref[c, s * seq_rows + 2:s * seq_rows + HALO_PAD, :] = halo_ref[s, :, c * 128:(c + 1) * 128]
        else:
            @pl.when(i % tiles_per_seq == 0)
            def _():
                for c in range(C_CONV // 128):
                    g_ref[c, 2:HALO_PAD, :] = halo_ref[0, :, c * 128:(c + 1) * 128]
        glu_step(0)

    @pl.when(j == STEP_AG1)
    def _():
        glu_step(COL_TILE // 2)

    @pl.when(j == STEP_ZC)
    def _():
        zc_ref[...] = _silu(u)
        rb = min(tt, CONV_ROWS)
        n_lane_tiles = C_CONV // 128

        def conv_block(base, r0):
            accs = []
            for c in range(n_lane_tiles):
                cs = slice(c * 128, (c + 1) * 128)
                acc = jnp.broadcast_to(cb_ref[:, cs], (rb, 128))
                for t in range(CONV_WIDTH):
                    acc = acc + cw_ref[t:t + 1, cs] * g_ref[c, pl.ds(base + r0 + 2 + t, rb), :]
                accs.append(acc)
            tot = accs[0]
            for c in range(1, n_lane_tiles):
                tot = tot + accs[c]
            mu = jnp.sum(tot, axis=-1, keepdims=True) * (1.0 / C_CONV)
            cen = [a - mu for a in accs]
            sq = cen[0] * cen[0]
            for c in range(1, n_lane_tiles):
                sq = sq + cen[c] * cen[c]
            rstd = lax.rsqrt(jnp.sum(sq, axis=-1, keepdims=True) * (1.0 / C_CONV) + EPS)
            return [a * rstd for a in cen]

        def finish_block(base, out_r0, r0):
            ys = conv_block(base, r0)
            for c in range(n_lane_tiles):
                cs = slice(c * 128, (c + 1) * 128)
                y = _silu(ys[c] * lng_ref[:, cs] + lnb_ref[:, cs])
                rows = pl.ds(out_r0 + r0, rb)
                cvb_ref[rows, cs] = (y * zc_ref[rows, cs]).astype(_BF16)

        for s in range(ns):
            base = s * seq_rows
            if tt == rb:
                finish_block(base, s * tt, 0)
            else:
                def body(b, carry, base=base, s=s):
                    finish_block(base, s * tt, pl.multiple_of(b * rb, rb))
                    return carry
                lax.fori_loop(0, tt // rb, body, 0)

        if tiles_per_seq == 1:
            for s in range(ns):
                for c in range(n_lane_tiles):
                    tail_ref[s, :, c * 128:(c + 1) * 128] = g_ref[c, (s + 1) * seq_rows - HALO:(s + 1) * seq_rows, :]
        else:
            @pl.when(i % tiles_per_seq == tiles_per_seq - 1)
            def _():
                for c in range(n_lane_tiles):
                    tail_ref[0, :, c * 128:(c + 1) * 128] = g_ref[c, seq_rows - HALO:seq_rows, :]
            g_ref[:, 0:HALO_PAD, :] = g_ref[:, tt:tt + HALO_PAD, :]

    @pl.when((j == STEP_GC0) | (j == STEP_GC1))
    def _():
        gc_ref[...] = _sigmoid(u).astype(_BF16)

    @pl.when((j == STEP_GA0) | (j == STEP_GA1))
    def _():
        ga_ref[...] = _sigmoid(u).astype(_BF16)


def _project(x, halo, ng, w, qg, kg, cw, cb, lng, lnb, *, n_seq, seq_len, tm, kv_row_offset=0):
    rows = n_seq * seq_len
    if seq_len >= tm:
        assert seq_len % tm == 0
        ns, tt, tps = 1, tm, seq_len // tm
    else:
        assert tm % seq_len == 0 and rows % tm == 0
        ns, tt, tps = tm // seq_len, seq_len, 1
    n_tiles = rows // tm
    halo_bcast = halo.shape[0] == 1

    def seq_map(i, j):
        return (i // tps, 0, i % tps, 0) if tps > 1 else (i, 0, 0, 0)

    def halo_map(i, j):
        if halo_bcast:
            return (0, 0, 0)
        return (i // tps, 0, 0) if tps > 1 else (i, 0, 0)

    def tail_map(i, j):
        return (i // tps, 0, 0) if tps > 1 else (i, 0, 0)

    const2 = lambda i, j: (0, 0)
    head_blk = (ns, N_HEADS, tt, D_VHEAD)
    if kv_row_offset:
        assert tps > 1
        kv_spec = pl.BlockSpec(tuple(pl.Element(d) for d in head_blk),
                               lambda i, j: (i // tps, 0, pl.multiple_of(kv_row_offset + (i % tps) * tt, 8), 0))
    else:
        kv_spec = pl.BlockSpec(head_blk, seq_map)
    kv_rows = kv_row_offset + seq_len
    grid = (n_tiles, N_COL_STEPS)
    kern = functools.partial(_proj_kernel, ns=ns, tt=tt, tiles_per_seq=tps)
    out_shape = (
        jax.ShapeDtypeStruct((n_seq, N_HEADS, seq_len, D_VHEAD), _BF16),
        jax.ShapeDtypeStruct((n_seq, N_HEADS, kv_rows, D_VHEAD), _F32),
        jax.ShapeDtypeStruct((n_seq, N_HEADS, kv_rows, D_VHEAD), _F32),
        jax.ShapeDtypeStruct((rows, D_ATTN), _BF16),
        jax.ShapeDtypeStruct((rows, C_CONV), _BF16),
        jax.ShapeDtypeStruct((rows, D_MODEL), _BF16),
        jax.ShapeDtypeStruct((rows, D_MODEL), _BF16),
        jax.ShapeDtypeStruct((n_seq, HALO, C_CONV), _F32),
    )
    in_specs = [
        pl.BlockSpec((tm, D_MODEL), lambda i, j: (i, 0)),
        pl.BlockSpec((1 if halo_bcast or tps > 1 else ns, HALO, C_CONV), halo_map),
        pl.BlockSpec((1, D_MODEL), const2),
        pl.BlockSpec((D_MODEL, COL_TILE), lambda i, j: (0, j)),
        pl.BlockSpec((N_HEADS, D_VHEAD), const2),
        pl.BlockSpec((N_HEADS, D_VHEAD), const2),
        pl.BlockSpec((CONV_WIDTH, C_CONV), const2),
        pl.BlockSpec((1, C_CONV), const2),
        pl.BlockSpec((1, C_CONV), const2),
        pl.BlockSpec((1, C_CONV), const2),
    ]
    out_specs = (
        pl.BlockSpec(head_blk, seq_map),
        kv_spec,
        kv_spec,
        pl.BlockSpec((tm, D_ATTN), lambda i, j: (i, 0)),
        pl.BlockSpec((tm, C_CONV), lambda i, j: (i, 0)),
        pl.BlockSpec((tm, COL_TILE), lambda i, j: (i, jnp.clip(j - STEP_GC0, 0, 1))),
        pl.BlockSpec((tm, COL_TILE), lambda i, j: (i, jnp.clip(j - STEP_GA0, 0, 1))),
        pl.BlockSpec((1 if tps > 1 else ns, HALO, C_CONV), tail_map),
    )
    scratch = [
        pltpu.VMEM((tm, D_MODEL), _BF16),
        pltpu.VMEM((C_CONV // 128, ns * (HALO_PAD + tt), 128), _F32),
        pltpu.VMEM((tm, C_CONV), _F32),
    ]
    return pl.pallas_call(
        kern, grid=grid, in_specs=in_specs, out_specs=out_specs, out_shape=out_shape,
        scratch_shapes=scratch,
        compiler_params=pltpu.CompilerParams(
            dimension_semantics=("arbitrary", "arbitrary"), vmem_limit_bytes=VMEM_LIMIT),
        name="proj",
    )(x, halo, ng, w, qg, kg, cw, cb, lng, lnb)


def _fill_meta_kernel(km_ref, vm_ref, k_any, v_any, ko_ref, vo_ref):
    del k_any, v_any
    ko_ref[0] = km_ref[...]
    vo_ref[0] = vm_ref[...]


def _fill_meta(k, v, km, vm):
    n_b = k.shape[0]
    meta = pl.BlockSpec((N_HEADS, N_META, D_VHEAD), lambda b: (0, 0, 0))
    anyspec = pl.BlockSpec(memory_space=pl.ANY)
    out = pl.BlockSpec((1, N_HEADS, N_META, D_VHEAD), lambda b: (b, 0, 0, 0))
    return pl.pallas_call(
        _fill_meta_kernel, grid=(n_b,), in_specs=[meta, meta, anyspec, anyspec], out_specs=(out, out),
        out_shape=(jax.ShapeDtypeStruct(k.shape, k.dtype), jax.ShapeDtypeStruct(v.shape, v.dtype)),
        input_output_aliases={2: 0, 3: 1},
        compiler_params=pltpu.CompilerParams(dimension_semantics=("arbitrary",)),
        name="fill_meta",
    )(km, vm, k, v)


def _attn_kernel(lam_ref, q_ref, k_ref, v_ref, za_ref, bd0_ref, bd1_ref, bm_ref, g_ref,
                 o_ref, kb_ref, vt_ref, s_ref, p_ref, *, seq_len):
    T = ATT_TILE
    n_q = seq_len // T
    lam = lam_ref[0]
    kb_ref[...] = k_ref[0, 0, N_META:, :].astype(_BF16)
    vt_ref[...] = v_ref[0, 0, N_META:, :].T.astype(_BF16)
    kmb = k_ref[0, 0, 0:N_META, :].astype(_BF16)
    vmt = v_ref[0, 0, 0:N_META, :].T.astype(_BF16)
    lane = lax.broadcasted_iota(jnp.int32, (1, D_VHEAD), 1)
    lo = lane < D_HEAD
    nt = (((1,), (1,)), ((), ()))

    def col_reduce(x, op):
        return op(x.reshape(x.shape[0] // 8, 8, T), axis=0)

    items = [(i, mp) for i in range(n_q) for mp in range(2)]
    qs_cache, m8, sm, mx, l8, outs = {}, {}, {}, {}, {}, {}

    def q_of(w):
        i, mp = items[w]
        if i not in qs_cache:
            q = q_ref[0, 0, i * T:(i + 1) * T, :]
            zero = jnp.zeros_like(q)
            qs_cache[i] = (jnp.where(lo, q, zero), jnp.where(lo, zero, q))
        return qs_cache[i][mp]

    def score_tile(w, j):
        i, _ = items[w]
        s = lax.dot_general(kb_ref[j * T:(j + 1) * T, :], q_of(w), nt, preferred_element_type=_F32)
        if j == i:
            s = s + bd0_ref[0]
        elif j == i - 1:
            s = s + bd1_ref[0]
        s_ref[w % 2, j * T:(j + 1) * T, :] = s
        t8 = col_reduce(s, jnp.max)
        m8[w] = t8 if j == 0 else jnp.maximum(m8[w], t8)

    def score_finish(w):
        i, _ = items[w]
        s = lax.dot_general(kmb, q_of(w), nt, preferred_element_type=_F32)
        if i == 0:
            s = s + bm_ref[0]
        sm[w] = s
        mx[w] = jnp.maximum(jnp.max(m8[w], axis=0, keepdims=True), jnp.max(s, axis=0, keepdims=True))

    def exp_tile(w, j):
        p = jnp.exp(s_ref[w % 2, j * T:(j + 1) * T, :] - mx[w])
        p_ref[w % 2, j * T:(j + 1) * T, :] = p.astype(_BF16)
        t8 = col_reduce(p, jnp.sum)
        l8[w] = t8 if j == 0 else l8[w] + t8

    def value_matmul(w):
        i, _ = items[w]
        nk = (i + 1) * T
        pm = jnp.exp(sm[w] - mx[w])
        l = jnp.sum(l8[w], axis=0, keepdims=True) + jnp.sum(pm, axis=0, keepdims=True)
        acc = (jnp.dot(vt_ref[:, 0:nk], p_ref[w % 2, 0:nk, :], preferred_element_type=_F32)
               + jnp.dot(vmt, pm.astype(_BF16), preferred_element_type=_F32))
        outs[w] = acc / l

    def finalize(i):
        o = outs[2 * i] - lam * outs[2 * i + 1]
        ms = jnp.mean(o * o, axis=0, keepdims=True)
        y = o * lax.rsqrt(ms + EPS) * g_ref[...]
        yt = y.T
        o_ref[0, i * T:(i + 1) * T, :] = (yt * za_ref[0, i * T:(i + 1) * T, :].astype(_F32)).astype(_BF16)

    n_items = len(items)
    for j in range(items[0][0] + 1):
        score_tile(0, j)
    score_finish(0)
    for w in range(n_items):
        n_exp = items[w][0] + 1
        n_next = items[w + 1][0] + 1 if w + 1 < n_items else 0
        for j in range(max(n_exp, n_next)):
            if j < n_next:
                score_tile(w + 1, j)
            if j < n_exp:
                exp_tile(w, j)
        if n_next:
            score_finish(w + 1)
        value_matmul(w)
        if items[w][1] == 1:
            finalize(items[w][0])


def _attention(lam, q, k, v, za, bd0, bd1, bm, g):
    n_b, _, seq_len, _ = q.shape
    T = ATT_TILE
    assert seq_len % T == 0
    kern = functools.partial(_attn_kernel, seq_len=seq_len)
    bh = lambda b, h: (b, h, 0, 0)
    hd = lambda b, h: (h, 0, 0)
    in_specs = [
        pl.BlockSpec(memory_space=pltpu.SMEM),
        pl.BlockSpec((1, 1, seq_len, D_VHEAD), bh),
        pl.BlockSpec((1, 1, N_META + seq_len, D_VHEAD), bh),
        pl.BlockSpec((1, 1, N_META + seq_len, D_VHEAD), bh),
        pl.BlockSpec((1, seq_len, D_VHEAD), lambda b, h: (b, 0, h)),
        pl.BlockSpec((1, T, T), hd),
        pl.BlockSpec((1, T, T), hd),
        pl.BlockSpec((1, N_META, T), hd),
        pl.BlockSpec((D_VHEAD, T), lambda b, h: (0, 0)),
    ]
    return pl.pallas_call(
        kern, grid=(n_b, N_HEADS), in_specs=in_specs,
        out_specs=pl.BlockSpec((1, seq_len, D_VHEAD), lambda b, h: (b, 0, h)),
        out_shape=jax.ShapeDtypeStruct((n_b, seq_len, D_ATTN), _BF16),
        scratch_shapes=[
            pltpu.VMEM((seq_len, D_VHEAD), _BF16),
            pltpu.VMEM((D_VHEAD, seq_len), _BF16),
            pltpu.VMEM((2, seq_len, T), _F32),
            pltpu.VMEM((2, seq_len, T), _BF16),
        ],
        compiler_params=pltpu.CompilerParams(
            dimension_semantics=("arbitrary", "arbitrary"), vmem_limit_bytes=VMEM_LIMIT),
        name="attn",
    )(lam, q, k, v, za, bd0, bd1, bm, g)


def _decode_kernel(lam_ref, q_ref, kn_ref, vn_ref, ck_ref, cv_ref, za_ref, bc_ref, bn_ref, g_ref, o_ref):
    lam = lam_ref[0]
    lane = lax.broadcasted_iota(jnp.int32, (1, D_VHEAD), 1)
    lo = lane < D_HEAD
    nt = (((1,), (1,)), ((), ()))
    for h in range(N_HEADS):
        q = q_ref[0, h]
        zero = jnp.zeros_like(q)
        qs = (jnp.where(lo, q, zero), jnp.where(lo, zero, q))
        ck = ck_ref[0, 0, h].astype(_BF16)
        kn = kn_ref[0, h].astype(_BF16)
        ps = []
        for mp in range(2):
            sc = lax.dot_general(qs[mp], ck, nt, preferred_element_type=_F32) + bc_ref[h]
            sn = lax.dot_general(qs[mp], kn, nt, preferred_element_type=_F32) + bn_ref[h]
            m = jnp.maximum(jnp.max(sc, axis=-1, keepdims=True), jnp.max(sn, axis=-1, keepdims=True))
            ec = jnp.exp(sc - m)
            en = jnp.exp(sn - m)
            l = jnp.sum(ec, axis=-1, keepdims=True) + jnp.sum(en, axis=-1, keepdims=True)
            ps.append((ec / l, en / l))
        ac = (ps[0][0] - lam * ps[1][0]).astype(_BF16)
        an = (ps[0][1] - lam * ps[1][1]).astype(_BF16)
        o = (jnp.dot(ac, cv_ref[0, 0, h].astype(_BF16), preferred_element_type=_F32)
             + jnp.dot(an, vn_ref[0, h].astype(_BF16), preferred_element_type=_F32))
        ms = jnp.mean(o * o, axis=-1, keepdims=True)
        y = o * lax.rsqrt(ms + EPS) * g_ref[...]
        cs = slice(h * D_VHEAD, (h + 1) * D_VHEAD)
        o_ref[0, :, cs] = (y * za_ref[0, :, cs].astype(_F32)).astype(_BF16)


def _decode_attention(lam, q, kn, vn, ck, cv, za, bc, bn, g):
    n_b, _, t_s, _ = q.shape
    past = ck.shape[3]
    b4 = lambda b: (b, 0, 0, 0)
    c3 = lambda b: (0, 0, 0)
    in_specs = [
        pl.BlockSpec(memory_space=pltpu.SMEM),
        pl.BlockSpec((1, N_HEADS, t_s, D_VHEAD), b4),
        pl.BlockSpec((1, N_HEADS, t_s, D_VHEAD), b4),
        pl.BlockSpec((1, N_HEADS, t_s, D_VHEAD), b4),
        pl.BlockSpec((1, 1, N_HEADS, past, D_VHEAD), lambda b: (0, b, 0, 0, 0)),
        pl.BlockSpec((1, 1, N_HEADS, past, D_VHEAD), lambda b: (0, b, 0, 0, 0)),
        pl.BlockSpec((1, t_s, D_ATTN), lambda b: (b, 0, 0)),
        pl.BlockSpec((N_HEADS, t_s, past), c3),
        pl.BlockSpec((N_HEADS, t_s, t_s), c3),
        pl.BlockSpec((1, D_VHEAD), lambda b: (0, 0)),
    ]
    return pl.pallas_call(
        _decode_kernel, grid=(n_b,), in_specs=in_specs,
        out_specs=pl.BlockSpec((1, t_s, D_ATTN), lambda b: (b, 0, 0)),
        out_shape=jax.ShapeDtypeStruct((n_b, t_s, D_ATTN), _BF16),
        compiler_params=pltpu.CompilerParams(
            dimension_semantics=("arbitrary",), vmem_limit_bytes=VMEM_LIMIT),
        name="decode_attn",
    )(lam, q, kn, vn, ck, cv, za, bc, bn, g)


def _finish_kernel(x_ref, at_ref, cv_ref, gc_ref, ga_ref, wb0_ref, wb1_ref, wo_ref, y_ref):
    pc = jnp.dot(cv_ref[...], wb0_ref[...], preferred_element_type=_F32)
    pa = jnp.dot(at_ref[...], wb1_ref[...], preferred_element_type=_F32)
    merged = gc_ref[...].astype(_F32) * pc + ga_ref[...].astype(_F32) * pa
    y_ref[...] = x_ref[...] + jnp.dot(merged.astype(_BF16), wo_ref[...], preferred_element_type=_F32)


def _finish(x, at, cv, gc, ga, wb0, wb1, wo, *, tm):
    rows = x.shape[0]
    assert rows % tm == 0
    row = lambda i: (i, 0)
    const = lambda i: (0, 0)
    single = pl.Buffered(1)
    in_specs = [
        pl.BlockSpec((tm, D_MODEL), row),
        pl.BlockSpec((tm, D_ATTN), row),
        pl.BlockSpec((tm, C_CONV), row),
        pl.BlockSpec((tm, D_MODEL), row),
        pl.BlockSpec((tm, D_MODEL), row),
        pl.BlockSpec((C_CONV, D_MODEL), const, pipeline_mode=single),
        pl.BlockSpec((D_ATTN, D_MODEL), const, pipeline_mode=single),
        pl.BlockSpec((D_MODEL, D_MODEL), const, pipeline_mode=single),
    ]
    return pl.pallas_call(
        _finish_kernel, grid=(rows // tm,), in_specs=in_specs,
        out_specs=pl.BlockSpec((tm, D_MODEL), row),
        out_shape=jax.ShapeDtypeStruct((rows, D_MODEL), _F32),
        compiler_params=pltpu.CompilerParams(
            dimension_semantics=("arbitrary",), vmem_limit_bytes=VMEM_LIMIT),
        name="finish",
    )(x, at, cv, gc, ga, wb0, wb1, wo)


def _bucket_bias(rel, table):
    half = N_BUCKETS // 2
    max_exact = half // 2
    n = jnp.abs(rel)
    n_f = jnp.maximum(n, 1).astype(jnp.float32)
    large = max_exact + (jnp.log(n_f / max_exact) / math.log(MAX_DISTANCE / max_exact)
                         * (half - max_exact)).astype(jnp.int32)
    large = jnp.minimum(large, half - 1)
    bucket = jnp.where(rel > 0, half, 0) + jnp.where(n < max_exact, n, large)
    onehot = (bucket[..., None] == jnp.arange(N_BUCKETS)).astype(jnp.float32)
    return jnp.einsum('...b,bh->...h', onehot, table.astype(jnp.float32), precision=lax.Precision.HIGHEST)


def _permute_w_in(w):
    qk_w = N_HEADS * D_HEAD
    d = w.shape[0]

    def heads(c0):
        return w[:, c0:c0 + 2 * qk_w].reshape(d, 2, N_HEADS, D_HEAD).transpose(0, 2, 1, 3).reshape(d, 2 * qk_w)

    c_ag = 4 * qk_w + 2 * D_ATTN
    half = COL_TILE // 2
    ag = w[:, c_ag:c_ag + 2 * C_CONV].reshape(d, 2, C_CONV // half, half).transpose(0, 2, 1, 3).reshape(d, 2 * C_CONV)
    parts = [heads(0), heads(2 * qk_w), w[:, 4 * qk_w:c_ag], ag, w[:, c_ag + 2 * C_CONV:]]
    return jnp.concatenate(parts, axis=1).astype(_BF16)


def kernel(x_prompt, x_sample, cache_k, cache_v, state_conv, meta_tokens, rel_bias, norm_gain, w_in,
           q_norm_gain, k_norm_gain, lambda_qk, subln_gain, conv_w, conv_b, conv_ln_gain, conv_ln_bias,
           w_branch_out, w_out):
    n_b, seq, _ = x_prompt.shape
    n_s, t_s, _ = x_sample.shape
    past = cache_k.shape[3]
    assert w_in.shape[0] == 1 and t_s == N_META and seq % ATT_TILE == 0
    layer = 0
    lam_init = 0.8 - 0.6 * math.exp(-0.3 * layer)

    lq = lambda_qk[layer].astype(_F32)
    lam = (jnp.exp(jnp.sum(lq[0] * lq[1])) - jnp.exp(jnp.sum(lq[2] * lq[3])) + lam_init).reshape(1)

    w = _permute_w_in(w_in[layer])
    ng = norm_gain[layer].reshape(1, D_MODEL)
    qg = jnp.tile(q_norm_gain[layer], (1, 2))
    kg = jnp.tile(k_norm_gain[layer], (1, 2))
    cw = conv_w[layer]
    cb = conv_b[layer].reshape(1, C_CONV)
    lng = conv_ln_gain[layer].reshape(1, C_CONV)
    lnb = conv_ln_bias[layer].reshape(1, C_CONV)
    wb0 = w_branch_out[layer, 0].astype(_BF16)
    wb1 = w_branch_out[layer, 1].astype(_BF16)
    wo = w_out[layer].astype(_BF16)
    g_out = subln_gain[layer].astype(_F32) * (1.0 - lam_init)
    proj = functools.partial(_project, ng=ng, w=w, qg=qg, kg=kg, cw=cw, cb=cb, lng=lng, lnb=lnb)

    x_small = jnp.concatenate([meta_tokens.astype(_F32), x_sample.reshape(n_s * t_s, D_MODEL)], axis=0)
    halo_small = jnp.concatenate([jnp.zeros((1, HALO, C_CONV), _F32), state_conv[layer].astype(_F32)], axis=0)
    q_sm, k_sm, v_sm, za_sm, cvb_sm, gc_sm, ga_sm, tail_sm = proj(
        x_small, halo_small, n_seq=n_s + 1, seq_len=t_s, tm=(n_s + 1) * t_s)

    q_p, k_p, v_p, za_p, cvb_p, gc_p, ga_p, tail_p = proj(
        x_prompt.reshape(n_b * seq, D_MODEL), tail_sm[0:1], n_seq=n_b, seq_len=seq, tm=512,
        kv_row_offset=N_META)
    k_p, v_p = _fill_meta(k_p, v_p, k_sm[0], v_sm[0])

    T = ATT_TILE
    far = rel_bias[N_BUCKETS // 2 - 1].astype(_F32)
    r = jnp.arange(T)
    rel_d = r[:, None] - r[None, :]
    allowed = (r[:, None] // CHUNK) <= (r[None, :] // CHUNK)
    bd0 = jnp.where(allowed[None], jnp.transpose(_bucket_bias(rel_d, rel_bias) - far, (2, 0, 1)), NEG)
    bd1 = jnp.transpose(_bucket_bias(rel_d - T, rel_bias) - far, (2, 0, 1))
    rel_m = jnp.arange(N_META)[:, None] - (N_META + r[None, :])
    bm = jnp.transpose(_bucket_bias(rel_m, rel_bias) - far, (2, 0, 1))
    g_t = jnp.broadcast_to(g_out[:, None], (D_VHEAD, T))

    at_p = _attention(lam, q_p, k_p, v_p, za_p.reshape(n_b, seq, D_ATTN), bd0, bd1, bm, g_t)

    q_pos = past + jnp.arange(t_s)
    bias_c = jnp.transpose(_bucket_bias(jnp.arange(past)[None, :] - q_pos[:, None], rel_bias), (2, 0, 1))
    bias_n = jnp.transpose(_bucket_bias(q_pos[None, :] - q_pos[:, None], rel_bias), (2, 0, 1))
    at_s = _decode_attention(lam, q_sm[1:], k_sm[1:], v_sm[1:], cache_k, cache_v,
                             za_sm[t_s:].reshape(n_s, t_s, D_ATTN), bias_c, bias_n, g_out.reshape(1, D_VHEAD))

    y_p = _finish(x_prompt.reshape(n_b * seq, D_MODEL), at_p.reshape(n_b * seq, D_ATTN), cvb_p, gc_p, ga_p,
                  wb0, wb1, wo, tm=512)
    y_s = _finish(x_sample.reshape(n_s * t_s, D_MODEL), at_s.reshape(n_s * t_s, D_ATTN), cvb_sm[t_s:],
                  gc_sm[t_s:], ga_sm[t_s:], wb0, wb1, wo, tm=n_s * t_s)

    return (y_p.reshape(n_b, seq, D_MODEL), y_s.reshape(n_s, t_s, D_MODEL), k_p[None], v_p[None],
            tail_p[None], k_sm[1:][None], v_sm[1:][None], tail_sm[1:][None])
```

```python
import functools
import math

import jax
import jax.numpy as jnp
import numpy as np
from jax import lax
from jax.experimental import pallas as pl
from jax.experimental.pallas import tpu as pltpu

D_MODEL = 2048
N_HEADS = 8
D_HEAD = 64
D_VHEAD = 2 * D_HEAD
D_ATTN = N_HEADS * D_VHEAD
C_CONV = D_MODEL // 2
CONV_WIDTH = 31
HALO = CONV_WIDTH - 1
N_META = 16
CHUNK = 64
N_BUCKETS = 32
MAX_DISTANCE = 128
EPS = 1e-6
NEG = -1e30
SCALE = D_HEAD ** -0.5

COL_TILE = 512
N_COL_STEPS = 22
STEP_GLU, STEP_ZC, STEP_Q, STEP_K, STEP_V, STEP_ZA, STEP_GC, STEP_GA, STEP_END = 0, 4, 6, 8, 10, 12, 14, 18, 22
HALO_PAD = 32
CONV_ROWS = 32
ATT_TILE = 256
VMEM_LIMIT = 56 * 1024 * 1024

_BF16 = jnp.bfloat16
_F32 = jnp.float32


def _sigmoid(x):
    return 1.0 / (1.0 + jnp.exp(-x))


def _silu(x):
    return x * _sigmoid(x)


def _head_norm(u, gain_ref, h0, scale):
    lane = lax.broadcasted_iota(jnp.int32, (1, D_VHEAD), 1)
    lo = lane < D_HEAD
    outs = []
    for hl in range(u.shape[1] // D_VHEAD):
        xh = u[:, hl * D_VHEAD:(hl + 1) * D_VHEAD]
        sq = xh * xh
        s_lo = jnp.sum(jnp.where(lo, sq, 0.0), axis=-1, keepdims=True)
        s_hi = jnp.sum(jnp.where(lo, 0.0, sq), axis=-1, keepdims=True)
        r = jnp.where(lo, lax.rsqrt(s_lo * (1.0 / D_HEAD) + EPS), lax.rsqrt(s_hi * (1.0 / D_HEAD) + EPS))
        y = xh * r * gain_ref[h0 + hl:h0 + hl + 1, :]
        if scale != 1.0:
            y = y * scale
        outs.append(y)
    return outs


def _proj_kernel(x_ref, halo_ref, ng_ref, w_hbm, qg_ref, kg_ref, cw_ref, cb_ref, lng_ref, lnb_ref, *rest,
                 ns, tt, tiles_per_seq, kv_row_offset):
    kv_manual = kv_row_offset > 0
    if kv_manual:
        (km_ref, vm_ref, q_ref, k_hbm, v_hbm, za_ref, cvb_ref, gc_ref, ga_ref, tail_ref,
         h_ref, w_buf0, w_buf1, u_ref0, u_ref1, g_ref, zc_ref, y_ref, sem, k_stage, v_stage, kv_sem) = rest
    else:
        (q_ref, k_ref, v_ref, za_ref, cvb_ref, gc_ref, ga_ref, tail_ref,
         h_ref, w_buf0, w_buf1, u_ref0, u_ref1, g_ref, zc_ref, y_ref, sem) = rest
    i = pl.program_id(0)
    n_i = pl.num_programs(0)
    seq_rows = HALO_PAD + tt
    hpt = COL_TILE // D_VHEAD
    lpt = COL_TILE // 128
    n_lane_tiles = C_CONV // 128
    rb = min(tt, CONV_ROWS)

    w_buf = (w_buf0, w_buf1)
    u_ref = (u_ref0, u_ref1)

    def w_copy(step, slot):
        return pltpu.make_async_copy(w_hbm.at[step], w_buf[slot], sem.at[slot])

    @pl.when(i == 0)
    def _():
        w_copy(0, 0).start()
        w_copy(1, 1).start()

    if kv_manual:
        assert ns == 1
        seq_idx = i // tiles_per_seq
        row0 = pl.multiple_of(kv_row_offset + (i % tiles_per_seq) * tt, 8)

        def kv_copy(which):
            if which < 2:
                src, dst = (k_stage, k_hbm) if which == 0 else (v_stage, v_hbm)
                dst = dst.at[seq_idx, :, pl.ds(row0, tt), :]
            else:
                src, dst = (km_ref, k_hbm) if which == 2 else (vm_ref, v_hbm)
                dst = dst.at[seq_idx, :, pl.ds(0, kv_row_offset), :]
            return pltpu.make_async_copy(src, dst, kv_sem.at[which])

        @pl.when(i % tiles_per_seq == 0)
        def _():
            kv_copy(2).start()
            kv_copy(3).start()

    if tiles_per_seq == 1:
        for s in range(ns):
            for c in range(n_lane_tiles):
                g_ref[c, s * seq_rows + 2:s * seq_rows + HALO_PAD, :] = halo_ref[s, :, c * 128:(c + 1) * 128]
    else:
        @pl.when(i % tiles_per_seq == 0)
        def _():
            for c in range(n_lane_tiles):
                g_ref[c, 2:HALO_PAD, :] = halo_ref[0, :, c * 128:(c + 1) * 128]

    x = x_ref[...]
    ms = jnp.mean(x * x, axis=-1, keepdims=True)
    h_ref[...] = (x * lax.rsqrt(ms + EPS) * ng_ref[...]).astype(_BF16)

    def split(n, parts, align):
        cuts = [align * round(n * k / (parts * align)) for k in range(parts + 1)]
        cuts[0], cuts[-1] = 0, n
        return list(zip(cuts[:-1], cuts[1:]))

    tm = ns * tt
    mm_pieces = [(rs, cs) for rs in split(tm, 2, 16) for cs in split(COL_TILE, 2, 256)]
    row_quarters = split(tm, 4, max(tt if ns > 1 else 16, 16))

    def matmul_piece(step, piece):
        (r0, r1), (c0, c1) = piece
        slot = step % 2
        u_ref[slot][r0:r1, c0:c1] = jnp.dot(h_ref[r0:r1, :], w_buf[slot][:, c0:c1], preferred_element_type=_F32)

    def seq_pieces(r0, r1):
        out = []
        for s in range(ns):
            lo, hi = max(r0, s * tt), min(r1, (s + 1) * tt)
            if lo < hi:
                out.append((s, lo - s * tt, hi - s * tt))
        return out

    def epilogue(step, r0, r1):
        u = u_ref[step % 2][r0:r1, :]
        pieces = seq_pieces(r0, r1)

        def rows(s, lo, hi):
            return slice(s * tt + lo - r0, s * tt + hi - r0)

        if STEP_Q <= step < STEP_V:
            is_q = step < STEP_K
            h0 = (step - (STEP_Q if is_q else STEP_K)) * hpt
            ys = _head_norm(u, qg_ref if is_q else kg_ref, h0, SCALE if is_q else 1.0)
            for hl in range(hpt):
                for (s, lo, hi) in pieces:
                    if is_q:
                        q_ref[s, h0 + hl, lo:hi, :] = ys[hl][rows(s, lo, hi)].astype(_BF16)
                    elif kv_manual:
                        k_stage[h0 + hl, lo:hi, :] = ys[hl][rows(s, lo, hi)]
                    else:
                        k_ref[s, h0 + hl, lo:hi, :] = ys[hl][rows(s, lo, hi)]
        elif STEP_V <= step < STEP_ZA:
            h0 = (step - STEP_V) * hpt
            for hl in range(hpt):
                for (s, lo, hi) in pieces:
                    blk = u[rows(s, lo, hi), hl * D_VHEAD:(hl + 1) * D_VHEAD]
                    if kv_manual:
                        v_stage[h0 + hl, lo:hi, :] = blk
                    else:
                        v_ref[s, h0 + hl, lo:hi, :] = blk
        elif STEP_ZA <= step < STEP_GC:
            c0 = (step - STEP_ZA) * COL_TILE
            za_ref[r0:r1, c0:c0 + COL_TILE] = _silu(u).astype(_BF16)
        elif STEP_GLU <= step < STEP_ZC:
            k = step - STEP_GLU
            lt0 = (k // 2) * lpt
            gate = _sigmoid(u) if k % 2 else None
            for (s, lo, hi) in pieces:
                dst = slice(s * seq_rows + HALO_PAD + lo, s * seq_rows + HALO_PAD + hi)
                for c in range(lpt):
                    blk = u[rows(s, lo, hi), c * 128:(c + 1) * 128]
                    if gate is None:
                        g_ref[lt0 + c, dst, :] = blk
                    else:
                        g_ref[lt0 + c, dst, :] = g_ref[lt0 + c, dst, :] * gate[rows(s, lo, hi), c * 128:(c + 1) * 128]
        elif STEP_ZC <= step < STEP_Q:
            c0 = (step - STEP_ZC) * COL_TILE
            zc_ref[r0:r1, c0:c0 + COL_TILE] = _silu(u)
        elif STEP_GC <= step < STEP_GA:
            c0 = (step - STEP_GC) * COL_TILE
            gc_ref[r0:r1, c0:c0 + COL_TILE] = _sigmoid(u).astype(_BF16)
        else:
            assert STEP_GA <= step < STEP_END
            c0 = (step - STEP_GA) * COL_TILE
            ga_ref[r0:r1, c0:c0 + COL_TILE] = _sigmoid(u).astype(_BF16)

    def conv_unit(s, r0, c):
        cs = slice(c * 128, (c + 1) * 128)
        base = s * seq_rows + r0
        acc = jnp.broadcast_to(cb_ref[:, cs], (rb, 128))
        for r in range(8):
            taps = [(a, 8 * a + r - 2) for a in range(HALO_PAD // 8 + 1) if 0 <= 8 * a + r - 2 < CONV_WIDTH]
            phase = g_ref[c, base + r:base + r + 8 * taps[-1][0] + rb, :]
            for a, t in taps:
                acc = acc + cw_ref[t:t + 1, cs] * phase[8 * a:8 * a + rb]
        y_ref[s * tt + r0:s * tt + r0 + rb, cs] = acc

    def norm_unit(s, r0):
        rows = slice(s * tt + r0, s * tt + r0 + rb)
        ys = [y_ref[rows, c * 128:(c + 1) * 128] for c in range(n_lane_tiles)]
        tot = ys[0]
        for c in range(1, n_lane_tiles):
            tot = tot + ys[c]
        mu = jnp.sum(tot, axis=-1, keepdims=True) * (1.0 / C_CONV)
        cen = [a - mu for a in ys]
        sq = cen[0] * cen[0]
        for c in range(1, n_lane_tiles):
            sq = sq + cen[c] * cen[c]
        rstd = lax.rsqrt(jnp.sum(sq, axis=-1, keepdims=True) * (1.0 / C_CONV) + EPS)
        for c in range(n_lane_tiles):
            cs = slice(c * 128, (c + 1) * 128)
            y = _silu(cen[c] * rstd * lng_ref[:, cs] + lnb_ref[:, cs])
            cvb_ref[rows, cs] = (y * zc_ref[rows, cs]).astype(_BF16)

    blocks = [(s, r0) for s in range(ns) for r0 in range(0, tt, rb)]
    conv_steps = list(range(STEP_ZC, N_COL_STEPS - 2))
    per_conv = -(-len(blocks) // len(conv_steps))
    conv_at = {st: blocks[k * per_conv:(k + 1) * per_conv] for k, st in enumerate(conv_steps)}
    norm_steps = list(range(STEP_Q, N_COL_STEPS))
    norm_at = {st: [] for st in norm_steps}
    done_before = {st: sum(len(conv_at.get(t, [])) for t in range(STEP_ZC, st)) for st in norm_steps}
    per_norm = -(-len(blocks) // (len(norm_steps) - 1))
    n_normed = 0
    for st in norm_steps:
        take = min(per_norm, done_before[st] - n_normed) if st != norm_steps[-1] else len(blocks) - n_normed
        norm_at[st] = blocks[n_normed:n_normed + take]
        n_normed += take

    w_copy(0, 0).wait()
    for piece in mm_pieces:
        matmul_piece(0, piece)
    n_q = len(row_quarters)

    def vpu_chunk(step, k):
        units = [(s, r0, c) for (s, r0) in conv_at.get(step, []) for c in range(n_lane_tiles)]
        norms = norm_at.get(step, [])
        epilogue(step, *row_quarters[k])
        for unit in units[k * len(units) // n_q:(k + 1) * len(units) // n_q]:
            conv_unit(*unit)
        for blk in norms[k * len(norms) // n_q:(k + 1) * len(norms) // n_q]:
            norm_unit(*blk)
        if kv_manual and k == n_q - 1 and step in (STEP_V - 1, STEP_ZA - 1):
            kv_copy(0 if step == STEP_V - 1 else 1).start()

    carried = None
    for step in range(N_COL_STEPS):
        if step + 1 < N_COL_STEPS:
            if step + 2 < N_COL_STEPS:
                w_copy(step + 2, step % 2).start()
            w_copy(step + 1, (step + 1) % 2).wait()
        if carried is not None:
            vpu_chunk(*carried)
        for k in range(n_q):
            if step + 1 < N_COL_STEPS:
                matmul_piece(step + 1, mm_pieces[k])
                if k == n_q - 1:
                    carried = (step, k)
                    break
            vpu_chunk(step, k)

    if tiles_per_seq == 1:
        for s in range(ns):
            for c in range(n_lane_tiles):
                tail_ref[s, :, c * 128:(c + 1) * 128] = g_ref[c, (s + 1) * seq_rows - HALO:(s + 1) * seq_rows, :]
    else:
        @pl.when(i % tiles_per_seq == tiles_per_seq - 1)
        def _():
            for c in range(n_lane_tiles):
                tail_ref[0, :, c * 128:(c + 1) * 128] = g_ref[c, seq_rows - HALO:seq_rows, :]
        g_ref[:, 0:HALO_PAD, :] = g_ref[:, tt:tt + HALO_PAD, :]

    if kv_manual:
        kv_copy(0).wait()
        kv_copy(1).wait()

        @pl.when(i % tiles_per_seq == 0)
        def _():
            kv_copy(2).wait()
            kv_copy(3).wait()

    @pl.when(i + 1 < n_i)
    def _():
        w_copy(0, 0).start()
        w_copy(1, 1).start()


def _project(x, halo, ng, w, qg, kg, cw, cb, lng, lnb, *, n_seq, seq_len, tm, kv_front=None):
    kv_row_offset = kv_front[0].shape[1] if kv_front is not None else 0
    rows = n_seq * seq_len
    if seq_len >= tm:
        assert seq_len % tm == 0
        ns, tt, tps = 1, tm, seq_len // tm
    else:
        assert tm % seq_len == 0 and rows % tm == 0
        ns, tt, tps = tm // seq_len, seq_len, 1
    n_tiles = rows // tm
    halo_bcast = halo.shape[0] == 1

    def seq_map(i):
        return (i // tps, 0, i % tps, 0) if tps > 1 else (i, 0, 0, 0)

    def halo_map(i):
        if halo_bcast:
            return (0, 0, 0)
        return (i // tps, 0, 0) if tps > 1 else (i, 0, 0)

    def tail_map(i):
        return (i // tps, 0, 0) if tps > 1 else (i, 0, 0)

    const2 = lambda i: (0, 0)
    head_blk = (ns, N_HEADS, tt, D_VHEAD)
    if kv_row_offset:
        assert ns == 1
        kv_spec = pl.BlockSpec(memory_space=pl.ANY)
    else:
        kv_spec = pl.BlockSpec(head_blk, seq_map)
    kv_rows = kv_row_offset + seq_len
    row = lambda i: (i, 0)
    kern = functools.partial(_proj_kernel, ns=ns, tt=tt, tiles_per_seq=tps, kv_row_offset=kv_row_offset)
    out_shape = (
        jax.ShapeDtypeStruct((n_seq, N_HEADS, seq_len, D_VHEAD), _BF16),
        jax.ShapeDtypeStruct((n_seq, N_HEADS, kv_rows, D_VHEAD), _F32),
        jax.ShapeDtypeStruct((n_seq, N_HEADS, kv_rows, D_VHEAD), _F32),
        jax.ShapeDtypeStruct((rows, D_ATTN), _BF16),
        jax.ShapeDtypeStruct((rows, C_CONV), _BF16),
        jax.ShapeDtypeStruct((rows, D_MODEL), _BF16),
        jax.ShapeDtypeStruct((rows, D_MODEL), _BF16),
        jax.ShapeDtypeStruct((n_seq, HALO, C_CONV), _F32),
    )
    in_specs = [
        pl.BlockSpec((tm, D_MODEL), row),
        pl.BlockSpec((1 if halo_bcast or tps > 1 else ns, HALO, C_CONV), halo_map),
        pl.BlockSpec((1, D_MODEL), const2),
        pl.BlockSpec(memory_space=pl.ANY),
        pl.BlockSpec((N_HEADS, D_VHEAD), const2),
        pl.BlockSpec((N_HEADS, D_VHEAD), const2),
        pl.BlockSpec((CONV_WIDTH, C_CONV), const2),
        pl.BlockSpec((1, C_CONV), const2),
        pl.BlockSpec((1, C_CONV), const2),
        pl.BlockSpec((1, C_CONV), const2),
    ]
    operands = [x, halo, ng, w, qg, kg, cw, cb, lng, lnb]
    if kv_row_offset:
        front = pl.BlockSpec((N_HEADS, kv_row_offset, D_VHEAD), lambda i: (0, 0, 0))
        in_specs += [front, front]
        operands += list(kv_front)
    out_specs = (
        pl.BlockSpec(head_blk, seq_map),
        kv_spec,
        kv_spec,
        pl.BlockSpec((tm, D_ATTN), row),
        pl.BlockSpec((tm, C_CONV), row),
        pl.BlockSpec((tm, D_MODEL), row),
        pl.BlockSpec((tm, D_MODEL), row),
        pl.BlockSpec((1 if tps > 1 else ns, HALO, C_CONV), tail_map),
    )
    scratch = [
        pltpu.VMEM((tm, D_MODEL), _BF16),
        pltpu.VMEM((D_MODEL, COL_TILE), _BF16),
        pltpu.VMEM((D_MODEL, COL_TILE), _BF16),
        pltpu.VMEM((tm, COL_TILE), _F32),
        pltpu.VMEM((tm, COL_TILE), _F32),
        pltpu.VMEM((C_CONV // 128, ns * (HALO_PAD + tt), 128), _F32),
        pltpu.VMEM((tm, C_CONV), _F32),
        pltpu.VMEM((tm, C_CONV), _F32),
        pltpu.SemaphoreType.DMA((2,)),
    ]
    if kv_row_offset:
        scratch += [
            pltpu.VMEM((N_HEADS, tt, D_VHEAD), _F32),
            pltpu.VMEM((N_HEADS, tt, D_VHEAD), _F32),
            pltpu.SemaphoreType.DMA((4,)),
        ]
    return pl.pallas_call(
        kern, grid=(n_tiles,), in_specs=in_specs, out_specs=out_specs, out_shape=out_shape,
        scratch_shapes=scratch,
        compiler_params=pltpu.CompilerParams(
            dimension_semantics=("arbitrary",), vmem_limit_bytes=VMEM_LIMIT),
        name="proj",
    )(*operands)


def _attn_kernel(lam_ref, q_ref, k_ref, v_ref, za_ref, bd0_ref, bd1_ref, bm_ref, g_ref,
                 o_ref, kb_ref, vt_ref, s_ref, p_ref, *, seq_len):
    T = ATT_TILE
    n_q = seq_len // T
    lam = lam_ref[0]
    kb_ref[...] = k_ref[0, 0, N_META:, :].astype(_BF16)
    vt_ref[...] = v_ref[0, 0, N_META:, :].T.astype(_BF16)
    kmb = k_ref[0, 0, 0:N_META, :].astype(_BF16)
    vmt = v_ref[0, 0, 0:N_META, :].T.astype(_BF16)
    lane = lax.broadcasted_iota(jnp.int32, (1, D_VHEAD), 1)
    lo = lane < D_HEAD
    nt = (((1,), (1,)), ((), ()))

    def col_reduce(x, op):
        return op(x.reshape(x.shape[0] // 8, 8, T), axis=0)

    items = [(i, mp) for i in range(n_q) for mp in range(2)]
    qs_cache, m8, sm, mx, l8, outs = {}, {}, {}, {}, {}, {}

    def q_of(w):
        i, mp = items[w]
        if i not in qs_cache:
            q = q_ref[0, 0, i * T:(i + 1) * T, :]
            zero = jnp.zeros_like(q)
            qs_cache[i] = (jnp.where(lo, q, zero), jnp.where(lo, zero, q))
        return qs_cache[i][mp]

    def score_tile(w, j):
        i, _ = items[w]
        s = lax.dot_general(kb_ref[j * T:(j + 1) * T, :], q_of(w), nt, preferred_element_type=_F32)
        if j == i:
            s = s + bd0_ref[0]
        elif j == i - 1:
            s = s + bd1_ref[0]
        s_ref[w % 2, j * T:(j + 1) * T, :] = s
        t8 = col_reduce(s, jnp.max)
        m8[w] = t8 if j == 0 else jnp.maximum(m8[w], t8)

    def score_finish(w):
        i, _ = items[w]
        s = lax.dot_general(kmb, q_of(w), nt, preferred_element_type=_F32)
        if i == 0:
            s = s + bm_ref[0]
        sm[w] = s
        mx[w] = jnp.maximum(jnp.max(m8[w], axis=0, keepdims=True), jnp.max(s, axis=0, keepdims=True))

    def exp_tile(w, j):
        p = jnp.exp(s_ref[w % 2, j * T:(j + 1) * T, :] - mx[w])
        p_ref[w % 2, j * T:(j + 1) * T, :] = p.astype(_BF16)
        t8 = col_reduce(p, jnp.sum)
        l8[w] = t8 if j == 0 else l8[w] + t8

    def value_matmul(w):
        i, _ = items[w]
        nk = (i + 1) * T
        pm = jnp.exp(sm[w] - mx[w])
        l = jnp.sum(l8[w], axis=0, keepdims=True) + jnp.sum(pm, axis=0, keepdims=True)
        acc = (jnp.dot(vt_ref[:, 0:nk], p_ref[w % 2, 0:nk, :], preferred_element_type=_F32)
               + jnp.dot(vmt, pm.astype(_BF16), preferred_element_type=_F32))
        outs[w] = acc / l

    def finalize(i):
        o = outs[2 * i] - lam * outs[2 * i + 1]
        ms = jnp.mean(o * o, axis=0, keepdims=True)
        y = o * lax.rsqrt(ms + EPS) * g_ref[...]
        yt = y.T
        o_ref[0, i * T:(i + 1) * T, :] = (yt * za_ref[0, i * T:(i + 1) * T, :].astype(_F32)).astype(_BF16)

    n_items = len(items)
    for j in range(items[0][0] + 1):
        score_tile(0, j)
    score_finish(0)
    for w in range(n_items):
        n_exp = items[w][0] + 1
        n_next = items[w + 1][0] + 1 if w + 1 < n_items else 0
        for j in range(max(n_exp, n_next)):
            if j < n_next:
                score_tile(w + 1, j)
            if j < n_exp:
                exp_tile(w, j)
        if n_next:
            score_finish(w + 1)
        value_matmul(w)
        if items[w][1] == 1:
            finalize(items[w][0])


def _attention(lam, q, k, v, za, bd0, bd1, bm, g):
    n_b, _, seq_len, _ = q.shape
    T = ATT_TILE
    assert seq_len % T == 0
    kern = functools.partial(_attn_kernel, seq_len=seq_len)
    bh = lambda b, h: (b, h, 0, 0)
    hd = lambda b, h: (h, 0, 0)
    in_specs = [
        pl.BlockSpec(memory_space=pltpu.SMEM),
        pl.BlockSpec((1, 1, seq_len, D_VHEAD), bh),
        pl.BlockSpec((1, 1, N_META + seq_len, D_VHEAD), bh),
        pl.BlockSpec((1, 1, N_META + seq_len, D_VHEAD), bh),
        pl.BlockSpec((1, seq_len, D_VHEAD), lambda b, h: (b, 0, h)),
        pl.BlockSpec((1, T, T), hd),
        pl.BlockSpec((1, T, T), hd),
        pl.BlockSpec((1, N_META, T), hd),
        pl.BlockSpec((D_VHEAD, T), lambda b, h: (0, 0)),
    ]
    return pl.pallas_call(
        kern, grid=(n_b, N_HEADS), in_specs=in_specs,
        out_specs=pl.BlockSpec((1, seq_len, D_VHEAD), lambda b, h: (b, 0, h)),
        out_shape=jax.ShapeDtypeStruct((n_b, seq_len, D_ATTN), _BF16),
        scratch_shapes=[
            pltpu.VMEM((seq_len, D_VHEAD), _BF16),
            pltpu.VMEM((D_VHEAD, seq_len), _BF16),
            pltpu.VMEM((2, seq_len, T), _F32),
            pltpu.VMEM((2, seq_len, T), _BF16),
        ],
        compiler_params=pltpu.CompilerParams(
            dimension_semantics=("arbitrary", "arbitrary"), vmem_limit_bytes=VMEM_LIMIT),
        name="attn",
    )(lam, q, k, v, za, bd0, bd1, bm, g)


def _decode_kernel(lam_ref, q_ref, kn_ref, vn_ref, ck_ref, cv_ref, za_ref, bc_ref, bn_ref, g_ref, o_ref):
    lam = lam_ref[0]
    lane = lax.broadcasted_iota(jnp.int32, (1, D_VHEAD), 1)
    lo = lane < D_HEAD
    nt = (((1,), (1,)), ((), ()))
    for h in range(N_HEADS):
        q = q_ref[0, h]
        zero = jnp.zeros_like(q)
        qs = (jnp.where(lo, q, zero), jnp.where(lo, zero, q))
        ck = ck_ref[0, 0, h].astype(_BF16)
        kn = kn_ref[0, h].astype(_BF16)
        ps = []
        for mp in range(2):
            sc = lax.dot_general(qs[mp], ck, nt, preferred_element_type=_F32) + bc_ref[h]
            sn = lax.dot_general(qs[mp], kn, nt, preferred_element_type=_F32) + bn_ref[h]
            m = jnp.maximum(jnp.max(sc, axis=-1, keepdims=True), jnp.max(sn, axis=-1, keepdims=True))
            ec = jnp.exp(sc - m)
            en = jnp.exp(sn - m)
            l = jnp.sum(ec, axis=-1, keepdims=True) + jnp.sum(en, axis=-1, keepdims=True)
            ps.append((ec / l, en / l))
        ac = (ps[0][0] - lam * ps[1][0]).astype(_BF16)
        an = (ps[0][1] - lam * ps[1][1]).astype(_BF16)
        o = (jnp.dot(ac, cv_ref[0, 0, h].astype(_BF16), preferred_element_type=_F32)
             + jnp.dot(an, vn_ref[0, h].astype(_BF16), preferred_element_type=_F32))
        ms = jnp.mean(o * o, axis=-1, keepdims=True)
        y = o * lax.rsqrt(ms + EPS) * g_ref[...]
        cs = slice(h * D_VHEAD, (h + 1) * D_VHEAD)
        o_ref[0, :, cs] = (y * za_ref[0, :, cs].astype(_F32)).astype(_BF16)


def _decode_attention(lam, q, kn, vn, ck, cv, za, bc, bn, g):
    n_b, _, t_s, _ = q.shape
    past = ck.shape[3]
    b4 = lambda b: (b, 0, 0, 0)
    c3 = lambda b: (0, 0, 0)
    in_specs = [
        pl.BlockSpec(memory_space=pltpu.SMEM),
        pl.BlockSpec((1, N_HEADS, t_s, D_VHEAD), b4),
        pl.BlockSpec((1, N_HEADS, t_s, D_VHEAD), b4),
        pl.BlockSpec((1, N_HEADS, t_s, D_VHEAD), b4),
        pl.BlockSpec((1, 1, N_HEADS, past, D_VHEAD), lambda b: (0, b, 0, 0, 0)),
        pl.BlockSpec((1, 1, N_HEADS, past, D_VHEAD), lambda b: (0, b, 0, 0, 0)),
        pl.BlockSpec((1, t_s, D_ATTN), lambda b: (b, 0, 0)),
        pl.BlockSpec((N_HEADS, t_s, past), c3),
        pl.BlockSpec((N_HEADS, t_s, t_s), c3),
        pl.BlockSpec((1, D_VHEAD), lambda b: (0, 0)),
    ]
    return pl.pallas_call(
        _decode_kernel, grid=(n_b,), in_specs=in_specs,
        out_specs=pl.BlockSpec((1, t_s, D_ATTN), lambda b: (b, 0, 0)),
        out_shape=jax.ShapeDtypeStruct((n_b, t_s, D_ATTN), _BF16),
        compiler_params=pltpu.CompilerParams(
            dimension_semantics=("arbitrary",), vmem_limit_bytes=VMEM_LIMIT),
        name="decode_attn",
    )(lam, q, kn, vn, ck, cv, za, bc, bn, g)


def _finish_kernel(x_ref, at_ref, cv_ref, gc_ref, ga_ref, wb0_ref, wb1_ref, wo_ref, y_ref):
    pc = jnp.dot(cv_ref[...], wb0_ref[...], preferred_element_type=_F32)
    pa = jnp.dot(at_ref[...], wb1_ref[...], preferred_element_type=_F32)
    merged = gc_ref[...].astype(_F32) * pc + ga_ref[...].astype(_F32) * pa
    y_ref[...] = x_ref[...] + jnp.dot(merged.astype(_BF16), wo_ref[...], preferred_element_type=_F32)


def _finish(x, at, cv, gc, ga, wb0, wb1, wo, *, tm):
    rows = x.shape[0]
    assert rows % tm == 0
    row = lambda i: (i, 0)
    const = lambda i: (0, 0)
    single = pl.Buffered(1)
    in_specs = [
        pl.BlockSpec((tm, D_MODEL), row),
        pl.BlockSpec((tm, D_ATTN), row),
        pl.BlockSpec((tm, C_CONV), row),
        pl.BlockSpec((tm, D_MODEL), row),
        pl.BlockSpec((tm, D_MODEL), row),
        pl.BlockSpec((C_CONV, D_MODEL), const, pipeline_mode=single),
        pl.BlockSpec((D_ATTN, D_MODEL), const, pipeline_mode=single),
        pl.BlockSpec((D_MODEL, D_MODEL), const, pipeline_mode=single),
    ]
    return pl.pallas_call(
        _finish_kernel, grid=(rows // tm,), in_specs=in_specs,
        out_specs=pl.BlockSpec((tm, D_MODEL), row),
        out_shape=jax.ShapeDtypeStruct((rows, D_MODEL), _F32),
        compiler_params=pltpu.CompilerParams(
            dimension_semantics=("arbitrary",), vmem_limit_bytes=VMEM_LIMIT),
        name="finish",
    )(x, at, cv, gc, ga, wb0, wb1, wo)


def _bucket_bias(rel, table):
    half = N_BUCKETS // 2
    max_exact = half // 2
    n = jnp.abs(rel)
    n_f = jnp.maximum(n, 1).astype(jnp.float32)
    large = max_exact + (jnp.log(n_f / max_exact) / math.log(MAX_DISTANCE / max_exact)
                         * (half - max_exact)).astype(jnp.int32)
    large = jnp.minimum(large, half - 1)
    bucket = jnp.where(rel > 0, half, 0) + jnp.where(n < max_exact, n, large)
    onehot = (bucket[..., None] == jnp.arange(N_BUCKETS)).astype(jnp.float32)
    return jnp.einsum('...b,bh->...h', onehot, table.astype(jnp.float32), precision=lax.Precision.HIGHEST)


def _permute_w_in(w):
    qk_w = N_HEADS * D_HEAD
    d = w.shape[0]

    def heads(c0):
        return w[:, c0:c0 + 2 * qk_w].reshape(d, 2, N_HEADS, D_HEAD).transpose(0, 2, 1, 3).reshape(d, 2 * qk_w)

    c_ag = 4 * qk_w + 2 * D_ATTN
    half = COL_TILE
    ag = w[:, c_ag:c_ag + 2 * C_CONV].reshape(d, 2, C_CONV // half, half).transpose(0, 2, 1, 3).reshape(d, 2 * C_CONV)
    c_zc = c_ag + 2 * C_CONV
    parts = [ag, w[:, c_zc:c_zc + C_CONV], heads(0), heads(2 * qk_w), w[:, 4 * qk_w:c_ag], w[:, c_zc + C_CONV:]]
    w = jnp.concatenate(parts, axis=1).astype(_BF16)
    return w.reshape(d, N_COL_STEPS, COL_TILE).transpose(1, 0, 2)


def kernel(x_prompt, x_sample, cache_k, cache_v, state_conv, meta_tokens, rel_bias, norm_gain, w_in,
           q_norm_gain, k_norm_gain, lambda_qk, subln_gain, conv_w, conv_b, conv_ln_gain, conv_ln_bias,
           w_branch_out, w_out):
    n_b, seq, _ = x_prompt.shape
    n_s, t_s, _ = x_sample.shape
    past = cache_k.shape[3]
    assert w_in.shape[0] == 1 and t_s == N_META and seq % ATT_TILE == 0
    layer = 0
    lam_init = 0.8 - 0.6 * math.exp(-0.3 * layer)

    lq = lambda_qk[layer].astype(_F32)
    lam = (jnp.exp(jnp.sum(lq[0] * lq[1])) - jnp.exp(jnp.sum(lq[2] * lq[3])) + lam_init).reshape(1)

    w = _permute_w_in(w_in[layer])
    ng = norm_gain[layer].reshape(1, D_MODEL)
    qg = jnp.tile(q_norm_gain[layer], (1, 2))
    kg = jnp.tile(k_norm_gain[layer], (1, 2))
    cw = conv_w[layer]
    cb = conv_b[layer].reshape(1, C_CONV)
    lng = conv_ln_gain[layer].reshape(1, C_CONV)
    lnb = conv_ln_bias[layer].reshape(1, C_CONV)
    wb0 = w_branch_out[layer, 0].astype(_BF16)
    wb1 = w_branch_out[layer, 1].astype(_BF16)
    wo = w_out[layer].astype(_BF16)
    g_out = subln_gain[layer].astype(_F32) * (1.0 - lam_init)
    proj = functools.partial(_project, ng=ng, w=w, qg=qg, kg=kg, cw=cw, cb=cb, lng=lng, lnb=lnb)

    x_small = jnp.concatenate([meta_tokens.astype(_F32), x_sample.reshape(n_s * t_s, D_MODEL)], axis=0)
    halo_small = jnp.concatenate([jnp.zeros((1, HALO, C_CONV), _F32), state_conv[layer].astype(_F32)], axis=0)
    q_sm, k_sm, v_sm, za_sm, cvb_sm, gc_sm, ga_sm, tail_sm = proj(
        x_small, halo_small, n_seq=n_s + 1, seq_len=t_s, tm=(n_s + 1) * t_s)

    q_p, k_p, v_p, za_p, cvb_p, gc_p, ga_p, tail_p = proj(
        x_prompt.reshape(n_b * seq, D_MODEL), tail_sm[0:1], n_seq=n_b, seq_len=seq, tm=512,
        kv_front=(k_sm[0], v_sm[0]))

    T = ATT_TILE
    far = rel_bias[N_BUCKETS // 2 - 1].astype(_F32)
    r = jnp.arange(T)
    rel_d = r[:, None] - r[None, :]
    allowed = (r[:, None] // CHUNK) <= (r[None, :] // CHUNK)
    bd0 = jnp.where(allowed[None], jnp.transpose(_bucket_bias(rel_d, rel_bias) - far, (2, 0, 1)), NEG)
    bd1 = jnp.transpose(_bucket_bias(rel_d - T, rel_bias) - far, (2, 0, 1))
    rel_m = jnp.arange(N_META)[:, None] - (N_META + r[None, :])
    bm = jnp.transpose(_bucket_bias(rel_m, rel_bias) - far, (2, 0, 1))
    g_t = jnp.broadcast_to(g_out[:, None], (D_VHEAD, T))

    at_p = _attention(lam, q_p, k_p, v_p, za_p.reshape(n_b, seq, D_ATTN), bd0, bd1, bm, g_t)

    q_pos = past + jnp.arange(t_s)
    bias_c = jnp.transpose(_bucket_bias(jnp.arange(past)[None, :] - q_pos[:, None], rel_bias), (2, 0, 1))
    bias_n = jnp.transpose(_bucket_bias(q_pos[None, :] - q_pos[:, None], rel_bias), (2, 0, 1))
    at_s = _decode_attention(lam, q_sm[1:], k_sm[1:], v_sm[1:], cache_k, cache_v,
                             za_sm[t_s:].reshape(n_s, t_s, D_ATTN), bias_c, bias_n, g_out.reshape(1, D_VHEAD))

    y_p = _finish(x_prompt.reshape(n_b * seq, D_MODEL), at_p.reshape(n_b * seq, D_ATTN), cvb_p, gc_p, ga_p,
                  wb0, wb1, wo, tm=512)
    y_s = _finish(x_sample.reshape(n_s * t_s, D_MODEL), at_s.reshape(n_s * t_s, D_ATTN), cvb_sm[t_s:],
                  gc_sm[t_s:], ga_sm[t_s:], wb0, wb1, wo, tm=n_s * t_s)

    return (y_p.reshape(n_b, seq, D_MODEL), y_s.reshape(n_s, t_s, D_MODEL), k_p[None], v_p[None],
            tail_p[None], k_sm[1:][None], v_sm[1:][None], tail_sm[1:][None])
```

```python
import functools
import math

import jax
import jax.numpy as jnp
import numpy as np
from jax import lax
from jax.experimental import pallas as pl
from jax.experimental.pallas import tpu as pltpu

D_MODEL = 2048
N_HEADS = 8
D_HEAD = 64
D_VHEAD = 2 * D_HEAD
D_ATTN = N_HEADS * D_VHEAD
C_CONV = D_MODEL // 2
CONV_WIDTH = 31
HALO = CONV_WIDTH - 1
N_META = 16
CHUNK = 64
N_BUCKETS = 32
MAX_DISTANCE = 128
EPS = 1e-6
NEG = -1e30
SCALE = D_HEAD ** -0.5
LOG2E = math.log2(math.e)
Q_SCALE = SCALE * LOG2E

COL_TILE = 512
N_COL_STEPS = 22
STEP_GLU, STEP_ZC, STEP_Q, STEP_K, STEP_V, STEP_ZA, STEP_GC, STEP_GA, STEP_END = 0, 4, 6, 8, 10, 12, 14, 18, 22
N_W_SLOTS = 4
HALO_PAD = 32
CONV_ROWS = 32
ATT_TILE = 256
VMEM_LIMIT = 56 * 1024 * 1024

_BF16 = jnp.bfloat16
_F32 = jnp.float32


def _sigmoid(x):
    return 1.0 / (1.0 + jnp.exp(-x))


def _silu(x):
    return x * _sigmoid(x)


def _head_norm(u, gain_ref, h0, scale):
    lane = lax.broadcasted_iota(jnp.int32, (1, D_VHEAD), 1)
    lo = lane < D_HEAD
    outs = []
    for hl in range(u.shape[1] // D_VHEAD):
        xh = u[:, hl * D_VHEAD:(hl + 1) * D_VHEAD]
        sq = xh * xh
        s_lo = jnp.sum(jnp.where(lo, sq, 0.0), axis=-1, keepdims=True)
        s_hi = jnp.sum(jnp.where(lo, 0.0, sq), axis=-1, keepdims=True)
        r = jnp.where(lo, lax.rsqrt(s_lo * (1.0 / D_HEAD) + EPS), lax.rsqrt(s_hi * (1.0 / D_HEAD) + EPS))
        y = xh * r * gain_ref[h0 + hl:h0 + hl + 1, :]
        if scale != 1.0:
            y = y * scale
        outs.append(y)
    return outs


def _proj_kernel(x_ref, halo_ref, ng_ref, w_hbm, qg_ref, kg_ref, cw_ref, cb_ref, lng_ref, lnb_ref, *rest,
                 ns, tt, tiles_per_seq, kv_row_offset):
    kv_manual = kv_row_offset > 0
    if kv_manual:
        (km_ref, vm_ref, q_ref, k_hbm, v_hbm, za_ref, cvb_ref, gc_ref, ga_ref, tail_ref,
         h_ref, w_buf0, w_buf1, w_buf2, w_buf3, u_ref0, u_ref1, g_ref, zc_ref, y_ref, sem,
         k_stage, v_stage, kv_sem) = rest
    else:
        (q_ref, k_ref, v_ref, za_ref, cvb_ref, gc_ref, ga_ref, tail_ref,
         h_ref, w_buf0, w_buf1, w_buf2, w_buf3, u_ref0, u_ref1, g_ref, zc_ref, y_ref, sem) = rest
    i = pl.program_id(0)
    n_i = pl.num_programs(0)
    seq_rows = HALO_PAD + tt
    hpt = COL_TILE // D_VHEAD
    lpt = COL_TILE // 128
    n_lane_tiles = C_CONV // 128
    rb = min(tt, CONV_ROWS)

    w_buf = (w_buf0, w_buf1, w_buf2, w_buf3)
    u_ref = (u_ref0, u_ref1)

    def w_copy(step, slot):
        return pltpu.make_async_copy(w_hbm.at[step], w_buf[slot], sem.at[slot])

    @pl.when(i == 0)
    def _():
        for t in range(N_W_SLOTS):
            w_copy(t, t).start()

    if kv_manual:
        assert ns == 1
        seq_idx = i // tiles_per_seq
        row0 = pl.multiple_of(kv_row_offset + (i % tiles_per_seq) * tt, 8)

        def kv_copy(which):
            if which < 2:
                src, dst = (k_stage, k_hbm) if which == 0 else (v_stage, v_hbm)
                dst = dst.at[seq_idx, :, pl.ds(row0, tt), :]
            else:
                src, dst = (km_ref, k_hbm) if which == 2 else (vm_ref, v_hbm)
                dst = dst.at[seq_idx, :, pl.ds(0, kv_row_offset), :]
            return pltpu.make_async_copy(src, dst, kv_sem.at[which])

        @pl.when(i % tiles_per_seq == 0)
        def _():
            kv_copy(2).start()
            kv_copy(3).start()

    if tiles_per_seq == 1:
        for s in range(ns):
            for c in range(n_lane_tiles):
                g_ref[c, s * seq_rows + 2:s * seq_rows + HALO_PAD, :] = halo_ref[s, :, c * 128:(c + 1) * 128]
    else:
        @pl.when(i % tiles_per_seq == 0)
        def _():
            for c in range(n_lane_tiles):
                g_ref[c, 2:HALO_PAD, :] = halo_ref[0, :, c * 128:(c + 1) * 128]

    x = x_ref[...]
    ms = jnp.mean(x * x, axis=-1, keepdims=True)
    h_ref[...] = (x * lax.rsqrt(ms + EPS) * ng_ref[...]).astype(_BF16)

    def split(n, parts, align):
        cuts = [align * round(n * k / (parts * align)) for k in range(parts + 1)]
        cuts[0], cuts[-1] = 0, n
        return list(zip(cuts[:-1], cuts[1:]))

    tm = ns * tt
    mm_pieces = [(rs, cs) for rs in split(tm, 2, 16) for cs in split(COL_TILE, 2, 256)]
    row_quarters = split(tm, 4, max(tt if ns > 1 else 16, 16))

    def matmul_piece(step, piece):
        (r0, r1), (c0, c1) = piece
        u_ref[step % 2][r0:r1, c0:c1] = jnp.dot(h_ref[r0:r1, :], w_buf[step % N_W_SLOTS][:, c0:c1],
                                                preferred_element_type=_F32)

    def seq_pieces(r0, r1):
        out = []
        for s in range(ns):
            lo, hi = max(r0, s * tt), min(r1, (s + 1) * tt)
            if lo < hi:
                out.append((s, lo - s * tt, hi - s * tt))
        return out

    def epilogue(step, r0, r1):
        u = u_ref[step % 2][r0:r1, :]
        pieces = seq_pieces(r0, r1)

        def rows(s, lo, hi):
            return slice(s * tt + lo - r0, s * tt + hi - r0)

        if STEP_Q <= step < STEP_V:
            is_q = step < STEP_K
            h0 = (step - (STEP_Q if is_q else STEP_K)) * hpt
            ys = _head_norm(u, qg_ref if is_q else kg_ref, h0, Q_SCALE if is_q else 1.0)
            for hl in range(hpt):
                for (s, lo, hi) in pieces:
                    if is_q:
                        q_ref[s, h0 + hl, lo:hi, :] = ys[hl][rows(s, lo, hi)].astype(_BF16)
                    elif kv_manual:
                        k_stage[h0 + hl, lo:hi, :] = ys[hl][rows(s, lo, hi)]
                    else:
                        k_ref[s, h0 + hl, lo:hi, :] = ys[hl][rows(s, lo, hi)]
        elif STEP_V <= step < STEP_ZA:
            h0 = (step - STEP_V) * hpt
            for hl in range(hpt):
                for (s, lo, hi) in pieces:
                    blk = u[rows(s, lo, hi), hl * D_VHEAD:(hl + 1) * D_VHEAD]
                    if kv_manual:
                        v_stage[h0 + hl, lo:hi, :] = blk
                    else:
                        v_ref[s, h0 + hl, lo:hi, :] = blk
        elif STEP_ZA <= step < STEP_GC:
            c0 = (step - STEP_ZA) * COL_TILE
            za_ref[r0:r1, c0:c0 + COL_TILE] = _silu(u).astype(_BF16)
        elif STEP_GLU <= step < STEP_ZC:
            k = step - STEP_GLU
            lt0 = (k // 2) * lpt
            gate = _sigmoid(u) if k % 2 else None
            for (s, lo, hi) in pieces:
                dst = slice(s * seq_rows + HALO_PAD + lo, s * seq_rows + HALO_PAD + hi)
                for c in range(lpt):
                    blk = u[rows(s, lo, hi), c * 128:(c + 1) * 128]
                    if gate is None:
                        g_ref[lt0 + c, dst, :] = blk
                    else:
                        g_ref[lt0 + c, dst, :] = g_ref[lt0 + c, dst, :] * gate[rows(s, lo, hi), c * 128:(c + 1) * 128]
        elif STEP_ZC <= step < STEP_Q:
            c0 = (step - STEP_ZC) * COL_TILE
            zc_ref[r0:r1, c0:c0 + COL_TILE] = _silu(u)
        elif STEP_GC <= step < STEP_GA:
            c0 = (step - STEP_GC) * COL_TILE
            gc_ref[r0:r1, c0:c0 + COL_TILE] = _sigmoid(u).astype(_BF16)
        else:
            assert STEP_GA <= step < STEP_END
            c0 = (step - STEP_GA) * COL_TILE
            ga_ref[r0:r1, c0:c0 + COL_TILE] = _sigmoid(u).astype(_BF16)

    def conv_unit(s, r0, c):
        cs = slice(c * 128, (c + 1) * 128)
        base = s * seq_rows + r0
        acc = jnp.broadcast_to(cb_ref[:, cs], (rb, 128))
        for r in range(8):
            taps = [(a, 8 * a + r - 2) for a in range(HALO_PAD // 8 + 1) if 0 <= 8 * a + r - 2 < CONV_WIDTH]
            phase = g_ref[c, base + r:base + r + 8 * taps[-1][0] + rb, :]
            for a, t in taps:
                acc = acc + cw_ref[t:t + 1, cs] * phase[8 * a:8 * a + rb]
        y_ref[s * tt + r0:s * tt + r0 + rb, cs] = acc

    def norm_unit(s, r0):
        rows = slice(s * tt + r0, s * tt + r0 + rb)
        ys = [y_ref[rows, c * 128:(c + 1) * 128] for c in range(n_lane_tiles)]
        tot = ys[0]
        for c in range(1, n_lane_tiles):
            tot = tot + ys[c]
        mu = jnp.sum(tot, axis=-1, keepdims=True) * (1.0 / C_CONV)
        cen = [a - mu for a in ys]
        sq = cen[0] * cen[0]
        for c in range(1, n_lane_tiles):
            sq = sq + cen[c] * cen[c]
        rstd = lax.rsqrt(jnp.sum(sq, axis=-1, keepdims=True) * (1.0 / C_CONV) + EPS)
        for c in range(n_lane_tiles):
            cs = slice(c * 128, (c + 1) * 128)
            y = _silu(cen[c] * rstd * lng_ref[:, cs] + lnb_ref[:, cs])
            cvb_ref[rows, cs] = (y * zc_ref[rows, cs]).astype(_BF16)

    blocks = [(s, r0) for s in range(ns) for r0 in range(0, tt, rb)]
    vregs = rb // 8
    item_cost = {"conv": (2 * CONV_WIDTH + 2) * vregs, "norm": 18 * vregs * n_lane_tiles}
    norm_lag = 2

    def epilogue_cost(step):
        per_vreg = (11 if STEP_Q <= step < STEP_V else 0 if STEP_V <= step < STEP_ZA
                    else 5 if step % 2 or step >= STEP_ZC else 0)
        return per_vreg * tm * COL_TILE // 1024

    items = []
    for bi in range(len(blocks) + norm_lag):
        if bi < len(blocks):
            items += [("conv",) + blocks[bi] + (c,) for c in range(n_lane_tiles)]
        if bi >= norm_lag:
            items.append(("norm",) + blocks[bi - norm_lag])
    work_steps = list(range(STEP_ZC, N_COL_STEPS))
    target = (sum(item_cost[it[0]] for it in items) + sum(map(epilogue_cost, work_steps))) / len(work_steps)
    work_at, pos = {}, 0
    for st in work_steps:
        budget, first = target - epilogue_cost(st), pos
        while pos < len(items) and (budget > 0 or st == work_steps[-1]):
            if items[pos][0] == "norm" and st < STEP_Q:
                break
            budget -= item_cost[items[pos][0]]
            pos += 1
        work_at[st] = items[first:pos]
    assert pos == len(items)

    def w_slot_ready(step):
        w_copy(step, step % N_W_SLOTS).wait()

    w_slot_ready(0)
    w_slot_ready(1)
    for piece in mm_pieces:
        matmul_piece(0, piece)
    n_q = len(row_quarters)

    def vpu_chunk(step, k):
        work = work_at.get(step, [])
        epilogue(step, *row_quarters[k])
        for it in work[k * len(work) // n_q:(k + 1) * len(work) // n_q]:
            if it[0] == "conv":
                conv_unit(*it[1:])
            else:
                norm_unit(*it[1:])
        if kv_manual and k == n_q - 1 and step in (STEP_V - 1, STEP_ZA - 1):
            kv_copy(0 if step == STEP_V - 1 else 1).start()

    carried = None
    for step in range(N_COL_STEPS):
        if step % 2 == 1:
            for t in (step + 1, step + 2):
                if t < N_COL_STEPS:
                    w_slot_ready(t)
            for t in (step + 3, step + 4):
                if t < N_COL_STEPS:
                    w_copy(t, t % N_W_SLOTS).start()
        if carried is not None:
            vpu_chunk(*carried)
        for k in range(n_q):
            if step + 1 < N_COL_STEPS:
                matmul_piece(step + 1, mm_pieces[k])
                if k == n_q - 1:
                    carried = (step, k)
                    break
            vpu_chunk(step, k)

    if tiles_per_seq == 1:
        for s in range(ns):
            for c in range(n_lane_tiles):
                tail_ref[s, :, c * 128:(c + 1) * 128] = g_ref[c, (s + 1) * seq_rows - HALO:(s + 1) * seq_rows, :]
    else:
        @pl.when(i % tiles_per_seq == tiles_per_seq - 1)
        def _():
            for c in range(n_lane_tiles):
                tail_ref[0, :, c * 128:(c + 1) * 128] = g_ref[c, seq_rows - HALO:seq_rows, :]
        g_ref[:, 0:HALO_PAD, :] = g_ref[:, tt:tt + HALO_PAD, :]

    if kv_manual:
        kv_copy(0).wait()
        kv_copy(1).wait()

        @pl.when(i % tiles_per_seq == 0)
        def _():
            kv_copy(2).wait()
            kv_copy(3).wait()

    @pl.when(i + 1 < n_i)
    def _():
        for t in range(N_W_SLOTS):
            w_copy(t, t).start()


def _project(x, halo, ng, w, qg, kg, cw, cb, lng, lnb, *, n_seq, seq_len, tm, kv_front=None):
    kv_row_offset = kv_front[0].shape[1] if kv_front is not None else 0
    rows = n_seq * seq_len
    if seq_len >= tm:
        assert seq_len % tm == 0
        ns, tt, tps = 1, tm, seq_len // tm
    else:
        assert tm % seq_len == 0 and rows % tm == 0
        ns, tt, tps = tm // seq_len, seq_len, 1
    n_tiles = rows // tm
    halo_bcast = halo.shape[0] == 1

    def seq_map(i):
        return (i // tps, 0, i % tps, 0) if tps > 1 else (i, 0, 0, 0)

    def halo_map(i):
        if halo_bcast:
            return (0, 0, 0)
        return (i // tps, 0, 0) if tps > 1 else (i, 0, 0)

    def tail_map(i):
        return (i // tps, 0, 0) if tps > 1 else (i, 0, 0)

    const2 = lambda i: (0, 0)
    head_blk = (ns, N_HEADS, tt, D_VHEAD)
    if kv_row_offset:
        assert ns == 1
        kv_spec = pl.BlockSpec(memory_space=pl.ANY)
    else:
        kv_spec = pl.BlockSpec(head_blk, seq_map)
    kv_rows = kv_row_offset + seq_len
    row = lambda i: (i, 0)
    kern = functools.partial(_proj_kernel, ns=ns, tt=tt, tiles_per_seq=tps, kv_row_offset=kv_row_offset)
    out_shape = (
        jax.ShapeDtypeStruct((n_seq, N_HEADS, seq_len, D_VHEAD), _BF16),
        jax.ShapeDtypeStruct((n_seq, N_HEADS, kv_rows, D_VHEAD), _F32),
        jax.ShapeDtypeStruct((n_seq, N_HEADS, kv_rows, D_VHEAD), _F32),
        jax.ShapeDtypeStruct((rows, D_ATTN), _BF16),
        jax.ShapeDtypeStruct((rows, C_CONV), _BF16),
        jax.ShapeDtypeStruct((rows, D_MODEL), _BF16),
        jax.ShapeDtypeStruct((rows, D_MODEL), _BF16),
        jax.ShapeDtypeStruct((n_seq, HALO, C_CONV), _F32),
    )
    in_specs = [
        pl.BlockSpec((tm, D_MODEL), row),
        pl.BlockSpec((1 if halo_bcast or tps > 1 else ns, HALO, C_CONV), halo_map),
        pl.BlockSpec((1, D_MODEL), const2),
        pl.BlockSpec(memory_space=pl.ANY),
        pl.BlockSpec((N_HEADS, D_VHEAD), const2),
        pl.BlockSpec((N_HEADS, D_VHEAD), const2),
        pl.BlockSpec((CONV_WIDTH, C_CONV), const2),
        pl.BlockSpec((1, C_CONV), const2),
        pl.BlockSpec((1, C_CONV), const2),
        pl.BlockSpec((1, C_CONV), const2),
    ]
    operands = [x, halo, ng, w, qg, kg, cw, cb, lng, lnb]
    if kv_row_offset:
        front = pl.BlockSpec((N_HEADS, kv_row_offset, D_VHEAD), lambda i: (0, 0, 0))
        in_specs += [front, front]
        operands += list(kv_front)
    out_specs = (
        pl.BlockSpec(head_blk, seq_map),
        kv_spec,
        kv_spec,
        pl.BlockSpec((tm, D_ATTN), row),
        pl.BlockSpec((tm, C_CONV), row),
        pl.BlockSpec((tm, D_MODEL), row),
        pl.BlockSpec((tm, D_MODEL), row),
        pl.BlockSpec((1 if tps > 1 else ns, HALO, C_CONV), tail_map),
    )
    scratch = [
        pltpu.VMEM((tm, D_MODEL), _BF16),
    ] + [pltpu.VMEM((D_MODEL, COL_TILE), _BF16) for _ in range(N_W_SLOTS)] + [
        pltpu.VMEM((tm, COL_TILE), _F32),
        pltpu.VMEM((tm, COL_TILE), _F32),
        pltpu.VMEM((C_CONV // 128, ns * (HALO_PAD + tt), 128), _F32),
        pltpu.VMEM((tm, C_CONV), _F32),
        pltpu.VMEM((tm, C_CONV), _F32),
        pltpu.SemaphoreType.DMA((N_W_SLOTS,)),
    ]
    if kv_row_offset:
        scratch += [
            pltpu.VMEM((N_HEADS, tt, D_VHEAD), _F32),
            pltpu.VMEM((N_HEADS, tt, D_VHEAD), _F32),
            pltpu.SemaphoreType.DMA((4,)),
        ]
    return pl.pallas_call(
        kern, grid=(n_tiles,), in_specs=in_specs, out_specs=out_specs, out_shape=out_shape,
        scratch_shapes=scratch,
        compiler_params=pltpu.CompilerParams(
            dimension_semantics=("arbitrary",), vmem_limit_bytes=VMEM_LIMIT),
        name="proj",
    )(*operands)


def _attn_kernel(lam_ref, q_ref, k_ref, v_ref, za_ref, bd0_ref, bd1_ref, bm_ref, g_ref,
                 o_ref, kb_ref, vt_ref, s_ref, p_ref, *, seq_len):
    T = ATT_TILE
    n_q = seq_len // T
    lam = lam_ref[0]
    kb_ref[...] = k_ref[0, 0, N_META:, :].astype(_BF16)
    vt_ref[...] = v_ref[0, 0, N_META:, :].T.astype(_BF16)
    kmb = k_ref[0, 0, 0:N_META, :].astype(_BF16)
    vmt = v_ref[0, 0, 0:N_META, :].T.astype(_BF16)
    lane = lax.broadcasted_iota(jnp.int32, (1, D_VHEAD), 1)
    lo = lane < D_HEAD
    nt = (((1,), (1,)), ((), ()))

    def col_reduce(x, op):
        return op(x.reshape(x.shape[0] // 8, 8, T), axis=0)

    items = [(i, mp) for i in range(n_q) for mp in range(2)]
    qs_cache, m8, sm, mx, l8, outs = {}, {}, {}, {}, {}, {}

    def q_of(w):
        i, mp = items[w]
        if i not in qs_cache:
            q = q_ref[0, 0, i * T:(i + 1) * T, :]
            zero = jnp.zeros_like(q)
            qs_cache[i] = (jnp.where(lo, q, zero), jnp.where(lo, zero, q))
        return qs_cache[i][mp]

    def score_tile(w, j):
        i, _ = items[w]
        s = lax.dot_general(kb_ref[j * T:(j + 1) * T, :], q_of(w), nt, preferred_element_type=_F32)
        if j == i:
            s = s + bd0_ref[0]
        elif j == i - 1:
            s = s + bd1_ref[0]
        s_ref[w % 2, j * T:(j + 1) * T, :] = s
        t8 = col_reduce(s, jnp.max)
        m8[w] = t8 if j == 0 else jnp.maximum(m8[w], t8)

    def score_finish(w):
        i, _ = items[w]
        s = lax.dot_general(kmb, q_of(w), nt, preferred_element_type=_F32)
        if i == 0:
            s = s + bm_ref[0]
        sm[w] = s
        mx[w] = jnp.maximum(jnp.max(m8[w], axis=0, keepdims=True), jnp.max(s, axis=0, keepdims=True))

    def exp_tile(w, j):
        p = jnp.exp2(s_ref[w % 2, j * T:(j + 1) * T, :] - mx[w])
        p_ref[w % 2, j * T:(j + 1) * T, :] = p.astype(_BF16)
        t8 = col_reduce(p, jnp.sum)
        l8[w] = t8 if j == 0 else l8[w] + t8

    def value_matmul(w):
        i, _ = items[w]
        nk = (i + 1) * T
        pm = jnp.exp2(sm[w] - mx[w])
        l = jnp.sum(l8[w], axis=0, keepdims=True) + jnp.sum(pm, axis=0, keepdims=True)
        acc = (jnp.dot(vt_ref[:, 0:nk], p_ref[w % 2, 0:nk, :], preferred_element_type=_F32)
               + jnp.dot(vmt, pm.astype(_BF16), preferred_element_type=_F32))
        outs[w] = acc / l

    def finalize(i):
        o = outs[2 * i] - lam * outs[2 * i + 1]
        ms = jnp.mean(o * o, axis=0, keepdims=True)
        y = o * lax.rsqrt(ms + EPS) * g_ref[...]
        yt = y.T
        o_ref[0, i * T:(i + 1) * T, :] = (yt * za_ref[0, i * T:(i + 1) * T, :].astype(_F32)).astype(_BF16)

    n_items = len(items)
    for j in range(items[0][0] + 1):
        score_tile(0, j)
    score_finish(0)
    for w in range(n_items):
        n_exp = items[w][0] + 1
        n_next = items[w + 1][0] + 1 if w + 1 < n_items else 0
        for j in range(max(n_exp, n_next)):
            if j < n_next:
                score_tile(w + 1, j)
            if j < n_exp:
                exp_tile(w, j)
        if n_next:
            score_finish(w + 1)
        value_matmul(w)
        if items[w][1] == 1:
            finalize(items[w][0])


def _attention(lam, q, k, v, za, bd0, bd1, bm, g):
    n_b, _, seq_len, _ = q.shape
    T = ATT_TILE
    assert seq_len % T == 0
    kern = functools.partial(_attn_kernel, seq_len=seq_len)
    bh = lambda b, h: (b, h, 0, 0)
    hd = lambda b, h: (h, 0, 0)
    in_specs = [
        pl.BlockSpec(memory_space=pltpu.SMEM),
        pl.BlockSpec((1, 1, seq_len, D_VHEAD), bh),
        pl.BlockSpec((1, 1, N_META + seq_len, D_VHEAD), bh),
        pl.BlockSpec((1, 1, N_META + seq_len, D_VHEAD), bh),
        pl.BlockSpec((1, seq_len, D_VHEAD), lambda b, h: (b, 0, h)),
        pl.BlockSpec((1, T, T), hd),
        pl.BlockSpec((1, T, T), hd),
        pl.BlockSpec((1, N_META, T), hd),
        pl.BlockSpec((D_VHEAD, T), lambda b, h: (0, 0)),
    ]
    return pl.pallas_call(
        kern, grid=(n_b, N_HEADS), in_specs=in_specs,
        out_specs=pl.BlockSpec((1, seq_len, D_VHEAD), lambda b, h: (b, 0, h)),
        out_shape=jax.ShapeDtypeStruct((n_b, seq_len, D_ATTN), _BF16),
        scratch_shapes=[
            pltpu.VMEM((seq_len, D_VHEAD), _BF16),
            pltpu.VMEM((D_VHEAD, seq_len), _BF16),
            pltpu.VMEM((2, seq_len, T), _F32),
            pltpu.VMEM((2, seq_len, T), _BF16),
        ],
        compiler_params=pltpu.CompilerParams(
            dimension_semantics=("arbitrary", "arbitrary"), vmem_limit_bytes=VMEM_LIMIT),
        name="attn",
    )(lam, q, k, v, za, bd0, bd1, bm, g)


def _decode_kernel(lam_ref, q_ref, kn_ref, vn_ref, ck_ref, cv_ref, za_ref, bc_ref, bn_ref, g_ref, o_ref):
    lam = lam_ref[0]
    lane = lax.broadcasted_iota(jnp.int32, (1, D_VHEAD), 1)
    lo = lane < D_HEAD
    nt = (((1,), (1,)), ((), ()))
    for h in range(N_HEADS):
        q = q_ref[0, h]
        zero = jnp.zeros_like(q)
        qs = (jnp.where(lo, q, zero), jnp.where(lo, zero, q))
        ck = ck_ref[0, 0, h].astype(_BF16)
        kn = kn_ref[0, h].astype(_BF16)
        ps = []
        for mp in range(2):
            sc = lax.dot_general(qs[mp], ck, nt, preferred_element_type=_F32) + bc_ref[h]
            sn = lax.dot_general(qs[mp], kn, nt, preferred_element_type=_F32) + bn_ref[h]
            m = jnp.maximum(jnp.max(sc, axis=-1, keepdims=True), jnp.max(sn, axis=-1, keepdims=True))
            ec = jnp.exp2(sc - m)
            en = jnp.exp2(sn - m)
            l = jnp.sum(ec, axis=-1, keepdims=True) + jnp.sum(en, axis=-1, keepdims=True)
            ps.append((ec / l, en / l))
        ac = (ps[0][0] - lam * ps[1][0]).astype(_BF16)
        an = (ps[0][1] - lam * ps[1][1]).astype(_BF16)
        o = (jnp.dot(ac, cv_ref[0, 0, h].astype(_BF16), preferred_element_type=_F32)
             + jnp.dot(an, vn_ref[0, h].astype(_BF16), preferred_element_type=_F32))
        ms = jnp.mean(o * o, axis=-1, keepdims=True)
        y = o * lax.rsqrt(ms + EPS) * g_ref[...]
        cs = slice(h * D_VHEAD, (h + 1) * D_VHEAD)
        o_ref[0, :, cs] = (y * za_ref[0, :, cs].astype(_F32)).astype(_BF16)


def _decode_attention(lam, q, kn, vn, ck, cv, za, bc, bn, g):
    n_b, _, t_s, _ = q.shape
    past = ck.shape[3]
    b4 = lambda b: (b, 0, 0, 0)
    c3 = lambda b: (0, 0, 0)
    in_specs = [
        pl.BlockSpec(memory_space=pltpu.SMEM),
        pl.BlockSpec((1, N_HEADS, t_s, D_VHEAD), b4),
        pl.BlockSpec((1, N_HEADS, t_s, D_VHEAD), b4),
        pl.BlockSpec((1, N_HEADS, t_s, D_VHEAD), b4),
        pl.BlockSpec((1, 1, N_HEADS, past, D_VHEAD), lambda b: (0, b, 0, 0, 0)),
        pl.BlockSpec((1, 1, N_HEADS, past, D_VHEAD), lambda b: (0, b, 0, 0, 0)),
        pl.BlockSpec((1, t_s, D_ATTN), lambda b: (b, 0, 0)),
        pl.BlockSpec((N_HEADS, t_s, past), c3),
        pl.BlockSpec((N_HEADS, t_s, t_s), c3),
        pl.BlockSpec((1, D_VHEAD), lambda b: (0, 0)),
    ]
    return pl.pallas_call(
        _decode_kernel, grid=(n_b,), in_specs=in_specs,
        out_specs=pl.BlockSpec((1, t_s, D_ATTN), lambda b: (b, 0, 0)),
        out_shape=jax.ShapeDtypeStruct((n_b, t_s, D_ATTN), _BF16),
        compiler_params=pltpu.CompilerParams(
            dimension_semantics=("arbitrary",), vmem_limit_bytes=VMEM_LIMIT),
        name="decode_attn",
    )(lam, q, kn, vn, ck, cv, za, bc, bn, g)


def _finish_kernel(x_ref, at_ref, cv_ref, gc_ref, ga_ref, wb0_ref, wb1_ref, wo_ref, y_ref):
    pc = jnp.dot(cv_ref[...], wb0_ref[...], preferred_element_type=_F32)
    pa = jnp.dot(at_ref[...], wb1_ref[...], preferred_element_type=_F32)
    merged = gc_ref[...].astype(_F32) * pc + ga_ref[...].astype(_F32) * pa
    y_ref[...] = x_ref[...] + jnp.dot(merged.astype(_BF16), wo_ref[...], preferred_element_type=_F32)


def _finish(x, at, cv, gc, ga, wb0, wb1, wo, *, tm):
    rows = x.shape[0]
    assert rows % tm == 0
    row = lambda i: (i, 0)
    const = lambda i: (0, 0)
    single = pl.Buffered(1)
    in_specs = [
        pl.BlockSpec((tm, D_MODEL), row),
        pl.BlockSpec((tm, D_ATTN), row),
        pl.BlockSpec((tm, C_CONV), row),
        pl.BlockSpec((tm, D_MODEL), row),
        pl.BlockSpec((tm, D_MODEL), row),
        pl.BlockSpec((C_CONV, D_MODEL), const, pipeline_mode=single),
        pl.BlockSpec((D_ATTN, D_MODEL), const, pipeline_mode=single),
        pl.BlockSpec((D_MODEL, D_MODEL), const, pipeline_mode=single),
    ]
    return pl.pallas_call(
        _finish_kernel, grid=(rows // tm,), in_specs=in_specs,
        out_specs=pl.BlockSpec((tm, D_MODEL), row),
        out_shape=jax.ShapeDtypeStruct((rows, D_MODEL), _F32),
        compiler_params=pltpu.CompilerParams(
            dimension_semantics=("arbitrary",), vmem_limit_bytes=VMEM_LIMIT),
        name="finish",
    )(x, at, cv, gc, ga, wb0, wb1, wo)


def _bucket_bias(rel, table):
    half = N_BUCKETS // 2
    max_exact = half // 2
    n = jnp.abs(rel)
    n_f = jnp.maximum(n, 1).astype(jnp.float32)
    large = max_exact + (jnp.log(n_f / max_exact) / math.log(MAX_DISTANCE / max_exact)
                         * (half - max_exact)).astype(jnp.int32)
    large = jnp.minimum(large, half - 1)
    bucket = jnp.where(rel > 0, half, 0) + jnp.where(n < max_exact, n, large)
    onehot = (bucket[..., None] == jnp.arange(N_BUCKETS)).astype(jnp.float32)
    return jnp.einsum('...b,bh->...h', onehot, table.astype(jnp.float32), precision=lax.Precision.HIGHEST)


def _permute_w_in(w):
    qk_w = N_HEADS * D_HEAD
    d = w.shape[0]

    def heads(c0):
        return w[:, c0:c0 + 2 * qk_w].reshape(d, 2, N_HEADS, D_HEAD).transpose(0, 2, 1, 3).reshape(d, 2 * qk_w)

    c_ag = 4 * qk_w + 2 * D_ATTN
    half = COL_TILE
    ag = w[:, c_ag:c_ag + 2 * C_CONV].reshape(d, 2, C_CONV // half, half).transpose(0, 2, 1, 3).reshape(d, 2 * C_CONV)
    c_zc = c_ag + 2 * C_CONV
    parts = [ag, w[:, c_zc:c_zc + C_CONV], heads(0), heads(2 * qk_w), w[:, 4 * qk_w:c_ag], w[:, c_zc + C_CONV:]]
    w = jnp.concatenate(parts, axis=1).astype(_BF16)
    return w.reshape(d, N_COL_STEPS, COL_TILE).transpose(1, 0, 2)


def kernel(x_prompt, x_sample, cache_k, cache_v, state_conv, meta_tokens, rel_bias, norm_gain, w_in,
           q_norm_gain, k_norm_gain, lambda_qk, subln_gain, conv_w, conv_b, conv_ln_gain, conv_ln_bias,
           w_branch_out, w_out):
    n_b, seq, _ = x_prompt.shape
    n_s, t_s, _ = x_sample.shape
    past = cache_k.shape[3]
    assert w_in.shape[0] == 1 and t_s == N_META and seq % ATT_TILE == 0
    layer = 0
    lam_init = 0.8 - 0.6 * math.exp(-0.3 * layer)

    lq = lambda_qk[layer].astype(_F32)
    lam = (jnp.exp(jnp.sum(lq[0] * lq[1])) - jnp.exp(jnp.sum(lq[2] * lq[3])) + lam_init).reshape(1)

    w = _permute_w_in(w_in[layer])
    ng = norm_gain[layer].reshape(1, D_MODEL)
    qg = jnp.tile(q_norm_gain[layer], (1, 2))
    kg = jnp.tile(k_norm_gain[layer], (1, 2))
    cw = conv_w[layer]
    cb = conv_b[layer].reshape(1, C_CONV)
    lng = conv_ln_gain[layer].reshape(1, C_CONV)
    lnb = conv_ln_bias[layer].reshape(1, C_CONV)
    wb0 = w_branch_out[layer, 0].astype(_BF16)
    wb1 = w_branch_out[layer, 1].astype(_BF16)
    wo = w_out[layer].astype(_BF16)
    g_out = subln_gain[layer].astype(_F32) * (1.0 - lam_init)
    proj = functools.partial(_project, ng=ng, w=w, qg=qg, kg=kg, cw=cw, cb=cb, lng=lng, lnb=lnb)

    x_small = jnp.concatenate([meta_tokens.astype(_F32), x_sample.reshape(n_s * t_s, D_MODEL)], axis=0)
    halo_small = jnp.concatenate([jnp.zeros((1, HALO, C_CONV), _F32), state_conv[layer].astype(_F32)], axis=0)
    q_sm, k_sm, v_sm, za_sm, cvb_sm, gc_sm, ga_sm, tail_sm = proj(
        x_small, halo_small, n_seq=n_s + 1, seq_len=t_s, tm=(n_s + 1) * t_s)

    q_p, k_p, v_p, za_p, cvb_p, gc_p, ga_p, tail_p = proj(
        x_prompt.reshape(n_b * seq, D_MODEL), tail_sm[0:1], n_seq=n_b, seq_len=seq, tm=512,
        kv_front=(k_sm[0], v_sm[0]))

    T = ATT_TILE
    far = rel_bias[N_BUCKETS // 2 - 1].astype(_F32)
    r = jnp.arange(T)
    rel_d = r[:, None] - r[None, :]
    allowed = (r[:, None] // CHUNK) <= (r[None, :] // CHUNK)
    bd0 = jnp.where(allowed[None], jnp.transpose(_bucket_bias(rel_d, rel_bias) - far, (2, 0, 1)) * LOG2E, NEG)
    bd1 = jnp.transpose(_bucket_bias(rel_d - T, rel_bias) - far, (2, 0, 1)) * LOG2E
    rel_m = jnp.arange(N_META)[:, None] - (N_META + r[None, :])
    bm = jnp.transpose(_bucket_bias(rel_m, rel_bias) - far, (2, 0, 1)) * LOG2E
    g_t = jnp.broadcast_to(g_out[:, None], (D_VHEAD, T))

    at_p = _attention(lam, q_p, k_p, v_p, za_p.reshape(n_b, seq, D_ATTN), bd0, bd1, bm, g_t)

    q_pos = past + jnp.arange(t_s)
    bias_c = jnp.transpose(_bucket_bias(jnp.arange(past)[None, :] - q_pos[:, None], rel_bias), (2, 0, 1)) * LOG2E
    bias_n = jnp.transpose(_bucket_bias(q_pos[None, :] - q_pos[:, None], rel_bias), (2, 0, 1)) * LOG2E
    at_s = _decode_attention(lam, q_sm[1:], k_sm[1:], v_sm[1:], cache_k, cache_v,
                             za_sm[t_s:].reshape(n_s, t_s, D_ATTN), bias_c, bias_n, g_out.reshape(1, D_VHEAD))

    y_p = _finish(x_prompt.reshape(n_b * seq, D_MODEL), at_p.reshape(n_b * seq, D_ATTN), cvb_p, gc_p, ga_p,
                  wb0, wb1, wo, tm=512)
    y_s = _finish(x_sample.reshape(n_s * t_s, D_MODEL), at_s.reshape(n_s * t_s, D_ATTN), cvb_sm[t_s:],
                  gc_sm[t_s:], ga_sm[t_s:], wb0, wb1, wo, tm=n_s * t_s)

    return (y_p.reshape(n_b, seq, D_MODEL), y_s.reshape(n_s, t_s, D_MODEL), k_p[None], v_p[None],
            tail_p[None], k_sm[1:][None], v_sm[1:][None], tail_sm[1:][None])
```

```python
import functools
import math

import jax
import jax.numpy as jnp
import numpy as np
from jax import lax
from jax.experimental import pallas as pl
from jax.experimental.pallas import tpu as pltpu

D_MODEL = 2048
N_HEADS = 8
D_HEAD = 64
D_VHEAD = 2 * D_HEAD
D_ATTN = N_HEADS * D_VHEAD
C_CONV = D_MODEL // 2
CONV_WIDTH = 31
HALO = CONV_WIDTH - 1
N_META = 16
CHUNK = 64
N_BUCKETS = 32
MAX_DISTANCE = 128
EPS = 1e-6
NEG = -1e30
SCALE = D_HEAD ** -0.5
LOG2E = math.log2(math.e)
Q_SCALE = SCALE * LOG2E

COL_TILE = 512
N_COL_STEPS = 22
STEP_GLU, STEP_ZC, STEP_Q, STEP_K, STEP_V, STEP_ZA, STEP_GC, STEP_GA, STEP_END = 0, 4, 6, 8, 10, 12, 14, 18, 22
N_W_RESIDENT = 6
N_W_SLOTS = 4
assert N_W_RESIDENT >= 1 and (N_COL_STEPS - N_W_RESIDENT - N_W_SLOTS) % 2 == 0
HALO_PAD = 32
CONV_ROWS = 32
ATT_TILE = 256
VMEM_LIMIT = 56 * 1024 * 1024
VMEM_LIMIT_PROJ = 60 * 1024 * 1024

_BF16 = jnp.bfloat16
_F32 = jnp.float32


def _sigmoid(x):
    return 1.0 / (1.0 + jnp.exp(-x))


def _silu(x):
    return x * _sigmoid(x)


def _head_norm(u, gain_ref, h0, scale):
    lane = lax.broadcasted_iota(jnp.int32, (1, D_VHEAD), 1)
    lo = lane < D_HEAD
    outs = []
    for hl in range(u.shape[1] // D_VHEAD):
        xh = u[:, hl * D_VHEAD:(hl + 1) * D_VHEAD]
        sq = xh * xh
        s_lo = jnp.sum(jnp.where(lo, sq, 0.0), axis=-1, keepdims=True)
        s_hi = jnp.sum(jnp.where(lo, 0.0, sq), axis=-1, keepdims=True)
        r = jnp.where(lo, lax.rsqrt(s_lo * (1.0 / D_HEAD) + EPS), lax.rsqrt(s_hi * (1.0 / D_HEAD) + EPS))
        y = xh * r * gain_ref[h0 + hl:h0 + hl + 1, :]
        if scale != 1.0:
            y = y * scale
        outs.append(y)
    return outs


def _proj_kernel(x_ref, halo_ref, ng_ref, w_hbm, qg_ref, kg_ref, cw_ref, cb_ref, lng_ref, lnb_ref, *rest,
                 ns, tt, tiles_per_seq, kv_row_offset):
    kv_manual = kv_row_offset > 0
    if kv_manual:
        (km_ref, vm_ref, q_ref, k_hbm, v_hbm, za_ref, cvb_ref, gc_ref, ga_ref, tail_ref,
         h_ref, *w_buf, u_ref0, u_ref1, g_ref, zc_ref, y_ref, sem, k_stage, v_stage, kv_sem) = rest
    else:
        (q_ref, k_ref, v_ref, za_ref, cvb_ref, gc_ref, ga_ref, tail_ref,
         h_ref, *w_buf, u_ref0, u_ref1, g_ref, zc_ref, y_ref, sem) = rest
    assert len(w_buf) == N_W_RESIDENT + N_W_SLOTS
    i = pl.program_id(0)
    n_i = pl.num_programs(0)
    seq_rows = HALO_PAD + tt
    hpt = COL_TILE // D_VHEAD
    lpt = COL_TILE // 128
    n_lane_tiles = C_CONV // 128
    rb = min(tt, CONV_ROWS)

    u_ref = (u_ref0, u_ref1)

    def w_index(step):
        return step if step < N_W_RESIDENT else N_W_RESIDENT + (step - N_W_RESIDENT) % N_W_SLOTS

    def w_copy(step):
        return pltpu.make_async_copy(w_hbm.at[step], w_buf[w_index(step)], sem.at[w_index(step)])

    @pl.when(i == 0)
    def _():
        for t in range(N_W_RESIDENT + N_W_SLOTS):
            w_copy(t).start()
        for t in range(N_W_RESIDENT):
            w_copy(t).wait()

    if kv_manual:
        assert ns == 1
        seq_idx = i // tiles_per_seq
        row0 = pl.multiple_of(kv_row_offset + (i % tiles_per_seq) * tt, 8)

        def kv_copy(which):
            if which < 2:
                src, dst = (k_stage, k_hbm) if which == 0 else (v_stage, v_hbm)
                dst = dst.at[seq_idx, :, pl.ds(row0, tt), :]
            else:
                src, dst = (km_ref, k_hbm) if which == 2 else (vm_ref, v_hbm)
                dst = dst.at[seq_idx, :, pl.ds(0, kv_row_offset), :]
            return pltpu.make_async_copy(src, dst, kv_sem.at[which])

        @pl.when(i % tiles_per_seq == 0)
        def _():
            kv_copy(2).start()
            kv_copy(3).start()

    if tiles_per_seq == 1:
        for s in range(ns):
            for c in range(n_lane_tiles):
                g_ref[c, s * seq_rows + 2:s * seq_rows + HALO_PAD, :] = halo_ref[s, :, c * 128:(c + 1) * 128]
    else:
        @pl.when(i % tiles_per_seq == 0)
        def _():
            for c in range(n_lane_tiles):
                g_ref[c, 2:HALO_PAD, :] = halo_ref[0, :, c * 128:(c + 1) * 128]

    x = x_ref[...]
    ms = jnp.mean(x * x, axis=-1, keepdims=True)
    h_ref[...] = (x * lax.rsqrt(ms + EPS) * ng_ref[...]).astype(_BF16)

    def split(n, parts, align):
        cuts = [align * round(n * k / (parts * align)) for k in range(parts + 1)]
        cuts[0], cuts[-1] = 0, n
        return list(zip(cuts[:-1], cuts[1:]))

    tm = ns * tt
    mm_pieces = [(rs, cs) for rs in split(tm, 2, 16) for cs in split(COL_TILE, 2, 256)]
    row_quarters = split(tm, 4, max(tt if ns > 1 else 16, 16))

    def matmul_piece(step, piece):
        (r0, r1), (c0, c1) = piece
        u_ref[step % 2][r0:r1, c0:c1] = jnp.dot(h_ref[r0:r1, :], w_buf[w_index(step)][:, c0:c1],
                                                preferred_element_type=_F32)

    def seq_pieces(r0, r1):
        out = []
        for s in range(ns):
            lo, hi = max(r0, s * tt), min(r1, (s + 1) * tt)
            if lo < hi:
                out.append((s, lo - s * tt, hi - s * tt))
        return out

    def epilogue(step, r0, r1):
        u = u_ref[step % 2][r0:r1, :]
        pieces = seq_pieces(r0, r1)

        def rows(s, lo, hi):
            return slice(s * tt + lo - r0, s * tt + hi - r0)

        if STEP_Q <= step < STEP_V:
            is_q = step < STEP_K
            h0 = (step - (STEP_Q if is_q else STEP_K)) * hpt
            ys = _head_norm(u, qg_ref if is_q else kg_ref, h0, Q_SCALE if is_q else 1.0)
            for hl in range(hpt):
                for (s, lo, hi) in pieces:
                    if is_q:
                        q_ref[s, h0 + hl, lo:hi, :] = ys[hl][rows(s, lo, hi)].astype(_BF16)
                    elif kv_manual:
                        k_stage[h0 + hl, lo:hi, :] = ys[hl][rows(s, lo, hi)]
                    else:
                        k_ref[s, h0 + hl, lo:hi, :] = ys[hl][rows(s, lo, hi)]
        elif STEP_V <= step < STEP_ZA:
            h0 = (step - STEP_V) * hpt
            for hl in range(hpt):
                for (s, lo, hi) in pieces:
                    blk = u[rows(s, lo, hi), hl * D_VHEAD:(hl + 1) * D_VHEAD]
                    if kv_manual:
                        v_stage[h0 + hl, lo:hi, :] = blk
                    else:
                        v_ref[s, h0 + hl, lo:hi, :] = blk
        elif STEP_ZA <= step < STEP_GC:
            c0 = (step - STEP_ZA) * COL_TILE
            za_ref[r0:r1, c0:c0 + COL_TILE] = _silu(u).astype(_BF16)
        elif STEP_GLU <= step < STEP_ZC:
            k = step - STEP_GLU
            lt0 = (k // 2) * lpt
            gate = _sigmoid(u) if k % 2 else None
            for (s, lo, hi) in pieces:
                dst = slice(s * seq_rows + HALO_PAD + lo, s * seq_rows + HALO_PAD + hi)
                for c in range(lpt):
                    blk = u[rows(s, lo, hi), c * 128:(c + 1) * 128]
                    if gate is None:
                        g_ref[lt0 + c, dst, :] = blk
                    else:
                        g_ref[lt0 + c, dst, :] = g_ref[lt0 + c, dst, :] * gate[rows(s, lo, hi), c * 128:(c + 1) * 128]
        elif STEP_ZC <= step < STEP_Q:
            c0 = (step - STEP_ZC) * COL_TILE
            zc_ref[r0:r1, c0:c0 + COL_TILE] = _silu(u)
        elif STEP_GC <= step < STEP_GA:
            c0 = (step - STEP_GC) * COL_TILE
            gc_ref[r0:r1, c0:c0 + COL_TILE] = _sigmoid(u).astype(_BF16)
        else:
            assert STEP_GA <= step < STEP_END
            c0 = (step - STEP_GA) * COL_TILE
            ga_ref[r0:r1, c0:c0 + COL_TILE] = _sigmoid(u).astype(_BF16)

    def conv_unit(s, r0, c):
        cs = slice(c * 128, (c + 1) * 128)
        base = s * seq_rows + r0
        acc = jnp.broadcast_to(cb_ref[:, cs], (rb, 128))
        for r in range(8):
            taps = [(a, 8 * a + r - 2) for a in range(HALO_PAD // 8 + 1) if 0 <= 8 * a + r - 2 < CONV_WIDTH]
            phase = g_ref[c, base + r:base + r + 8 * taps[-1][0] + rb, :]
            for a, t in taps:
                acc = acc + cw_ref[t:t + 1, cs] * phase[8 * a:8 * a + rb]
        y_ref[s * tt + r0:s * tt + r0 + rb, cs] = acc

    def norm_unit(s, r0):
        rows = slice(s * tt + r0, s * tt + r0 + rb)
        ys = [y_ref[rows, c * 128:(c + 1) * 128] for c in range(n_lane_tiles)]
        tot = ys[0]
        for c in range(1, n_lane_tiles):
            tot = tot + ys[c]
        mu = jnp.sum(tot, axis=-1, keepdims=True) * (1.0 / C_CONV)
        cen = [a - mu for a in ys]
        sq = cen[0] * cen[0]
        for c in range(1, n_lane_tiles):
            sq = sq + cen[c] * cen[c]
        rstd = lax.rsqrt(jnp.sum(sq, axis=-1, keepdims=True) * (1.0 / C_CONV) + EPS)
        for c in range(n_lane_tiles):
            cs = slice(c * 128, (c + 1) * 128)
            y = _silu(cen[c] * rstd * lng_ref[:, cs] + lnb_ref[:, cs])
            cvb_ref[rows, cs] = (y * zc_ref[rows, cs]).astype(_BF16)

    blocks = [(s, r0) for s in range(ns) for r0 in range(0, tt, rb)]
    vregs = rb // 8
    item_cost = {"conv": (2 * CONV_WIDTH + 2) * vregs, "norm": 18 * vregs * n_lane_tiles}
    norm_lag = 2

    def epilogue_cost(step):
        per_vreg = (11 if STEP_Q <= step < STEP_V else 0 if STEP_V <= step < STEP_ZA
                    else 5 if step % 2 or step >= STEP_ZC else 0)
        return per_vreg * tm * COL_TILE // 1024

    items = []
    for bi in range(len(blocks) + norm_lag):
        if bi < len(blocks):
            items += [("conv",) + blocks[bi] + (c,) for c in range(n_lane_tiles)]
        if bi >= norm_lag:
            items.append(("norm",) + blocks[bi - norm_lag])
    work_steps = list(range(STEP_ZC, N_COL_STEPS))
    target = (sum(item_cost[it[0]] for it in items) + sum(map(epilogue_cost, work_steps))) / len(work_steps)
    work_at, pos = {}, 0
    for st in work_steps:
        budget, first = target - epilogue_cost(st), pos
        while pos < len(items) and (budget > 0 or st == work_steps[-1]):
            if items[pos][0] == "norm" and st < STEP_Q:
                break
            budget -= item_cost[items[pos][0]]
            pos += 1
        work_at[st] = items[first:pos]
    assert pos == len(items)

    for piece in mm_pieces:
        matmul_piece(0, piece)
    n_q = len(row_quarters)

    def vpu_chunk(step, k):
        work = work_at.get(step, [])
        epilogue(step, *row_quarters[k])
        for it in work[k * len(work) // n_q:(k + 1) * len(work) // n_q]:
            if it[0] == "conv":
                conv_unit(*it[1:])
            else:
                norm_unit(*it[1:])
        if kv_manual and k == n_q - 1 and step in (STEP_V - 1, STEP_ZA - 1):
            kv_copy(0 if step == STEP_V - 1 else 1).start()

    carried = None
    for step in range(N_COL_STEPS):
        if step == N_W_RESIDENT - 1:
            w_copy(step + 1).wait()
            w_copy(step + 2).wait()
        elif step > N_W_RESIDENT and (step - N_W_RESIDENT) % 2 == 1:
            for t in (step + 1, step + 2):
                if t < N_COL_STEPS:
                    w_copy(t).wait()
            for t in (step + 3, step + 4):
                if t < N_COL_STEPS:
                    w_copy(t).start()
        if carried is not None:
            vpu_chunk(*carried)
        for k in range(n_q):
            if step + 1 < N_COL_STEPS:
                matmul_piece(step + 1, mm_pieces[k])
                if k == n_q - 1:
                    carried = (step, k)
                    break
            vpu_chunk(step, k)

    if tiles_per_seq == 1:
        for s in range(ns):
            for c in range(n_lane_tiles):
                tail_ref[s, :, c * 128:(c + 1) * 128] = g_ref[c, (s + 1) * seq_rows - HALO:(s + 1) * seq_rows, :]
    else:
        @pl.when(i % tiles_per_seq == tiles_per_seq - 1)
        def _():
            for c in range(n_lane_tiles):
                tail_ref[0, :, c * 128:(c + 1) * 128] = g_ref[c, seq_rows - HALO:seq_rows, :]
        g_ref[:, 0:HALO_PAD, :] = g_ref[:, tt:tt + HALO_PAD, :]

    if kv_manual:
        kv_copy(0).wait()
        kv_copy(1).wait()

        @pl.when(i % tiles_per_seq == 0)
        def _():
            kv_copy(2).wait()
            kv_copy(3).wait()

    @pl.when(i + 1 < n_i)
    def _():
        for t in range(N_W_RESIDENT, N_W_RESIDENT + N_W_SLOTS):
            w_copy(t).start()


def _project(x, halo, ng, w, qg, kg, cw, cb, lng, lnb, *, n_seq, seq_len, tm, kv_front=None):
    kv_row_offset = kv_front[0].shape[1] if kv_front is not None else 0
    rows = n_seq * seq_len
    if seq_len >= tm:
        assert seq_len % tm == 0
        ns, tt, tps = 1, tm, seq_len // tm
    else:
        assert tm % seq_len == 0 and rows % tm == 0
        ns, tt, tps = tm // seq_len, seq_len, 1
    n_tiles = rows // tm
    halo_bcast = halo.shape[0] == 1

    def seq_map(i):
        return (i // tps, 0, i % tps, 0) if tps > 1 else (i, 0, 0, 0)

    def halo_map(i):
        if halo_bcast:
            return (0, 0, 0)
        return (i // tps, 0, 0) if tps > 1 else (i, 0, 0)

    def tail_map(i):
        return (i // tps, 0, 0) if tps > 1 else (i, 0, 0)

    const2 = lambda i: (0, 0)
    head_blk = (ns, N_HEADS, tt, D_VHEAD)
    if kv_row_offset:
        assert ns == 1
        kv_spec = pl.BlockSpec(memory_space=pl.ANY)
    else:
        kv_spec = pl.BlockSpec(head_blk, seq_map)
    kv_rows = kv_row_offset + seq_len
    row = lambda i: (i, 0)
    kern = functools.partial(_proj_kernel, ns=ns, tt=tt, tiles_per_seq=tps, kv_row_offset=kv_row_offset)
    out_shape = (
        jax.ShapeDtypeStruct((n_seq, N_HEADS, seq_len, D_VHEAD), _BF16),
        jax.ShapeDtypeStruct((n_seq, N_HEADS, kv_rows, D_VHEAD), _F32),
        jax.ShapeDtypeStruct((n_seq, N_HEADS, kv_rows, D_VHEAD), _F32),
        jax.ShapeDtypeStruct((rows, D_ATTN), _BF16),
        jax.ShapeDtypeStruct((rows, C_CONV), _BF16),
        jax.ShapeDtypeStruct((rows, D_MODEL), _BF16),
        jax.ShapeDtypeStruct((rows, D_MODEL), _BF16),
        jax.ShapeDtypeStruct((n_seq, HALO, C_CONV), _F32),
    )
    in_specs = [
        pl.BlockSpec((tm, D_MODEL), row),
        pl.BlockSpec((1 if halo_bcast or tps > 1 else ns, HALO, C_CONV), halo_map),
        pl.BlockSpec((1, D_MODEL), const2),
        pl.BlockSpec(memory_space=pl.ANY),
        pl.BlockSpec((N_HEADS, D_VHEAD), const2),
        pl.BlockSpec((N_HEADS, D_VHEAD), const2),
        pl.BlockSpec((CONV_WIDTH, C_CONV), const2),
        pl.BlockSpec((1, C_CONV), const2),
        pl.BlockSpec((1, C_CONV), const2),
        pl.BlockSpec((1, C_CONV), const2),
    ]
    operands = [x, halo, ng, w, qg, kg, cw, cb, lng, lnb]
    if kv_row_offset:
        front = pl.BlockSpec((N_HEADS, kv_row_offset, D_VHEAD), lambda i: (0, 0, 0))
        in_specs += [front, front]
        operands += list(kv_front)
    out_specs = (
        pl.BlockSpec(head_blk, seq_map),
        kv_spec,
        kv_spec,
        pl.BlockSpec((tm, D_ATTN), row),
        pl.BlockSpec((tm, C_CONV), row),
        pl.BlockSpec((tm, D_MODEL), row),
        pl.BlockSpec((tm, D_MODEL), row),
        pl.BlockSpec((1 if tps > 1 else ns, HALO, C_CONV), tail_map),
    )
    scratch = [
        pltpu.VMEM((tm, D_MODEL), _BF16),
    ] + [pltpu.VMEM((D_MODEL, COL_TILE), _BF16) for _ in range(N_W_RESIDENT + N_W_SLOTS)] + [
        pltpu.VMEM((tm, COL_TILE), _F32),
        pltpu.VMEM((tm, COL_TILE), _F32),
        pltpu.VMEM((C_CONV // 128, ns * (HALO_PAD + tt), 128), _F32),
        pltpu.VMEM((tm, C_CONV), _F32),
        pltpu.VMEM((tm, C_CONV), _F32),
        pltpu.SemaphoreType.DMA((N_W_RESIDENT + N_W_SLOTS,)),
    ]
    if kv_row_offset:
        scratch += [
            pltpu.VMEM((N_HEADS, tt, D_VHEAD), _F32),
            pltpu.VMEM((N_HEADS, tt, D_VHEAD), _F32),
            pltpu.SemaphoreType.DMA((4,)),
        ]
    return pl.pallas_call(
        kern, grid=(n_tiles,), in_specs=in_specs, out_specs=out_specs, out_shape=out_shape,
        scratch_shapes=scratch,
        compiler_params=pltpu.CompilerParams(
            dimension_semantics=("arbitrary",), vmem_limit_bytes=VMEM_LIMIT_PROJ),
        name="proj",
    )(*operands)


def _attn_kernel(lam_ref, q_ref, k_ref, v_ref, za_ref, bd0_ref, bd1_ref, bm_ref, g_ref,
                 o_ref, kb_ref, vt_ref, s_ref, p_ref, *, seq_len):
    T = ATT_TILE
    n_q = seq_len // T
    lam = lam_ref[0]
    kb_ref[...] = k_ref[0, 0, N_META:, :].astype(_BF16)
    vt_ref[...] = v_ref[0, 0, N_META:, :].T.astype(_BF16)
    kmb = k_ref[0, 0, 0:N_META, :].astype(_BF16)
    vmt = v_ref[0, 0, 0:N_META, :].T.astype(_BF16)
    lane = lax.broadcasted_iota(jnp.int32, (1, D_VHEAD), 1)
    lo = lane < D_HEAD
    nt = (((1,), (1,)), ((), ()))

    def col_reduce(x, op):
        return op(x.reshape(x.shape[0] // 8, 8, T), axis=0)

    items = [(i, mp) for i in range(n_q) for mp in range(2)]
    qs_cache, m8, sm, mx, l8, outs = {}, {}, {}, {}, {}, {}

    def q_of(w):
        i, mp = items[w]
        if i not in qs_cache:
            q = q_ref[0, 0, i * T:(i + 1) * T, :]
            zero = jnp.zeros_like(q)
            qs_cache[i] = (jnp.where(lo, q, zero), jnp.where(lo, zero, q))
        return qs_cache[i][mp]

    def score_tile(w, j):
        i, _ = items[w]
        s = lax.dot_general(kb_ref[j * T:(j + 1) * T, :], q_of(w), nt, preferred_element_type=_F32)
        if j == i:
            s = s + bd0_ref[0]
        elif j == i - 1:
            s = s + bd1_ref[0]
        s_ref[w % 2, j * T:(j + 1) * T, :] = s
        t8 = col_reduce(s, jnp.max)
        m8[w] = t8 if j == 0 else jnp.maximum(m8[w], t8)

    def score_finish(w):
        i, _ = items[w]
        s = lax.dot_general(kmb, q_of(w), nt, preferred_element_type=_F32)
        if i == 0:
            s = s + bm_ref[0]
        sm[w] = s
        mx[w] = jnp.maximum(jnp.max(m8[w], axis=0, keepdims=True), jnp.max(s, axis=0, keepdims=True))

    def exp_tile(w, j):
        p = jnp.exp2(s_ref[w % 2, j * T:(j + 1) * T, :] - mx[w])
        p_ref[w % 2, j * T:(j + 1) * T, :] = p.astype(_BF16)
        t8 = col_reduce(p, jnp.sum)
        l8[w] = t8 if j == 0 else l8[w] + t8

    def value_matmul(w):
        i, _ = items[w]
        nk = (i + 1) * T
        pm = jnp.exp2(sm[w] - mx[w])
        l = jnp.sum(l8[w], axis=0, keepdims=True) + jnp.sum(pm, axis=0, keepdims=True)
        acc = (jnp.dot(vt_ref[:, 0:nk], p_ref[w % 2, 0:nk, :], preferred_element_type=_F32)
               + jnp.dot(vmt, pm.astype(_BF16), preferred_element_type=_F32))
        outs[w] = acc / l

    def finalize(i):
        o = outs[2 * i] - lam * outs[2 * i + 1]
        ms = jnp.mean(o * o, axis=0, keepdims=True)
        y = o * lax.rsqrt(ms + EPS) * g_ref[...]
        yt = y.T
        o_ref[0, i * T:(i + 1) * T, :] = (yt * za_ref[0, i * T:(i + 1) * T, :].astype(_F32)).astype(_BF16)

    n_items = len(items)
    for j in range(items[0][0] + 1):
        score_tile(0, j)
    score_finish(0)
    for w in range(n_items):
        n_exp = items[w][0] + 1
        n_next = items[w + 1][0] + 1 if w + 1 < n_items else 0
        for j in range(max(n_exp, n_next)):
            if j < n_next:
                score_tile(w + 1, j)
            if j < n_exp:
                exp_tile(w, j)
        if n_next:
            score_finish(w + 1)
        value_matmul(w)
        if items[w][1] == 1:
            finalize(items[w][0])


def _attention(lam, q, k, v, za, bd0, bd1, bm, g):
    n_b, _, seq_len, _ = q.shape
    T = ATT_TILE
    assert seq_len % T == 0
    kern = functools.partial(_attn_kernel, seq_len=seq_len)
    bh = lambda b, h: (b, h, 0, 0)
    hd = lambda b, h: (h, 0, 0)
    in_specs = [
        pl.BlockSpec(memory_space=pltpu.SMEM),
        pl.BlockSpec((1, 1, seq_len, D_VHEAD), bh),
        pl.BlockSpec((1, 1, N_META + seq_len, D_VHEAD), bh),
        pl.BlockSpec((1, 1, N_META + seq_len, D_VHEAD), bh),
        pl.BlockSpec((1, seq_len, D_VHEAD), lambda b, h: (b, 0, h)),
        pl.BlockSpec((1, T, T), hd),
        pl.BlockSpec((1, T, T), hd),
        pl.BlockSpec((1, N_META, T), hd),
        pl.BlockSpec((D_VHEAD, T), lambda b, h: (0, 0)),
    ]
    return pl.pallas_call(
        kern, grid=(n_b, N_HEADS), in_specs=in_specs,
        out_specs=pl.BlockSpec((1, seq_len, D_VHEAD), lambda b, h: (b, 0, h)),
        out_shape=jax.ShapeDtypeStruct((n_b, seq_len, D_ATTN), _BF16),
        scratch_shapes=[
            pltpu.VMEM((seq_len, D_VHEAD), _BF16),
            pltpu.VMEM((D_VHEAD, seq_len), _BF16),
            pltpu.VMEM((2, seq_len, T), _F32),
            pltpu.VMEM((2, seq_len, T), _BF16),
        ],
        compiler_params=pltpu.CompilerParams(
            dimension_semantics=("arbitrary", "arbitrary"), vmem_limit_bytes=VMEM_LIMIT),
        name="attn",
    )(lam, q, k, v, za, bd0, bd1, bm, g)


def _decode_kernel(lam_ref, q_ref, kn_ref, vn_ref, ck_ref, cv_ref, za_ref, bc_ref, bn_ref, g_ref, o_ref):
    lam = lam_ref[0]
    lane = lax.broadcasted_iota(jnp.int32, (1, D_VHEAD), 1)
    lo = lane < D_HEAD
    nt = (((1,), (1,)), ((), ()))
    for h in range(N_HEADS):
        q = q_ref[0, h]
        zero = jnp.zeros_like(q)
        qs = (jnp.where(lo, q, zero), jnp.where(lo, zero, q))
        ck = ck_ref[0, 0, h].astype(_BF16)
        kn = kn_ref[0, h].astype(_BF16)
        ps = []
        for mp in range(2):
            sc = lax.dot_general(qs[mp], ck, nt, preferred_element_type=_F32) + bc_ref[h]
            sn = lax.dot_general(qs[mp], kn, nt, preferred_element_type=_F32) + bn_ref[h]
            m = jnp.maximum(jnp.max(sc, axis=-1, keepdims=True), jnp.max(sn, axis=-1, keepdims=True))
            ec = jnp.exp2(sc - m)
            en = jnp.exp2(sn - m)
            l = jnp.sum(ec, axis=-1, keepdims=True) + jnp.sum(en, axis=-1, keepdims=True)
            ps.append((ec / l, en / l))
        ac = (ps[0][0] - lam * ps[1][0]).astype(_BF16)
        an = (ps[0][1] - lam * ps[1][1]).astype(_BF16)
        o = (jnp.dot(ac, cv_ref[0, 0, h].astype(_BF16), preferred_element_type=_F32)
             + jnp.dot(an, vn_ref[0, h].astype(_BF16), preferred_element_type=_F32))
        ms = jnp.mean(o * o, axis=-1, keepdims=True)
        y = o * lax.rsqrt(ms + EPS) * g_ref[...]
        cs = slice(h * D_VHEAD, (h + 1) * D_VHEAD)
        o_ref[0, :, cs] = (y * za_ref[0, :, cs].astype(_F32)).astype(_BF16)


def _decode_attention(lam, q, kn, vn, ck, cv, za, bc, bn, g):
    n_b, _, t_s, _ = q.shape
    past = ck.shape[3]
    b4 = lambda b: (b, 0, 0, 0)
    c3 = lambda b: (0, 0, 0)
    in_specs = [
        pl.BlockSpec(memory_space=pltpu.SMEM),
        pl.BlockSpec((1, N_HEADS, t_s, D_VHEAD), b4),
        pl.BlockSpec((1, N_HEADS, t_s, D_VHEAD), b4),
        pl.BlockSpec((1, N_HEADS, t_s, D_VHEAD), b4),
        pl.BlockSpec((1, 1, N_HEADS, past, D_VHEAD), lambda b: (0, b, 0, 0, 0)),
        pl.BlockSpec((1, 1, N_HEADS, past, D_VHEAD), lambda b: (0, b, 0, 0, 0)),
        pl.BlockSpec((1, t_s, D_ATTN), lambda b: (b, 0, 0)),
        pl.BlockSpec((N_HEADS, t_s, past), c3),
        pl.BlockSpec((N_HEADS, t_s, t_s), c3),
        pl.BlockSpec((1, D_VHEAD), lambda b: (0, 0)),
    ]
    return pl.pallas_call(
        _decode_kernel, grid=(n_b,), in_specs=in_specs,
        out_specs=pl.BlockSpec((1, t_s, D_ATTN), lambda b: (b, 0, 0)),
        out_shape=jax.ShapeDtypeStruct((n_b, t_s, D_ATTN), _BF16),
        compiler_params=pltpu.CompilerParams(
            dimension_semantics=("arbitrary",), vmem_limit_bytes=VMEM_LIMIT),
        name="decode_attn",
    )(lam, q, kn, vn, ck, cv, za, bc, bn, g)


def _finish_kernel(x_ref, at_ref, cv_ref, gc_ref, ga_ref, wb0_ref, wb1_ref, wo_ref, y_ref):
    pc = jnp.dot(cv_ref[...], wb0_ref[...], preferred_element_type=_F32)
    pa = jnp.dot(at_ref[...], wb1_ref[...], preferred_element_type=_F32)
    merged = gc_ref[...].astype(_F32) * pc + ga_ref[...].astype(_F32) * pa
    y_ref[...] = x_ref[...] + jnp.dot(merged.astype(_BF16), wo_ref[...], preferred_element_type=_F32)


def _finish(x, at, cv, gc, ga, wb0, wb1, wo, *, tm):
    rows = x.shape[0]
    assert rows % tm == 0
    row = lambda i: (i, 0)
    const = lambda i: (0, 0)
    single = pl.Buffered(1)
    in_specs = [
        pl.BlockSpec((tm, D_MODEL), row),
        pl.BlockSpec((tm, D_ATTN), row),
        pl.BlockSpec((tm, C_CONV), row),
        pl.BlockSpec((tm, D_MODEL), row),
        pl.BlockSpec((tm, D_MODEL), row),
        pl.BlockSpec((C_CONV, D_MODEL), const, pipeline_mode=single),
        pl.BlockSpec((D_ATTN, D_MODEL), const, pipeline_mode=single),
        pl.BlockSpec((D_MODEL, D_MODEL), const, pipeline_mode=single),
    ]
    return pl.pallas_call(
        _finish_kernel, grid=(rows // tm,), in_specs=in_specs,
        out_specs=pl.BlockSpec((tm, D_MODEL), row),
        out_shape=jax.ShapeDtypeStruct((rows, D_MODEL), _F32),
        compiler_params=pltpu.CompilerParams(
            dimension_semantics=("arbitrary",), vmem_limit_bytes=VMEM_LIMIT),
        name="finish",
    )(x, at, cv, gc, ga, wb0, wb1, wo)


def _bucket_bias(rel, table):
    half = N_BUCKETS // 2
    max_exact = half // 2
    n = jnp.abs(rel)
    n_f = jnp.maximum(n, 1).astype(jnp.float32)
    large = max_exact + (jnp.log(n_f / max_exact) / math.log(MAX_DISTANCE / max_exact)
                         * (half - max_exact)).astype(jnp.int32)
    large = jnp.minimum(large, half - 1)
    bucket = jnp.where(rel > 0, half, 0) + jnp.where(n < max_exact, n, large)
    onehot = (bucket[..., None] == jnp.arange(N_BUCKETS)).astype(jnp.float32)
    return jnp.einsum('...b,bh->...h', onehot, table.astype(jnp.float32), precision=lax.Precision.HIGHEST)


def _permute_w_in(w):
    qk_w = N_HEADS * D_HEAD
    d = w.shape[0]

    def heads(c0):
        return w[:, c0:c0 + 2 * qk_w].reshape(d, 2, N_HEADS, D_HEAD).transpose(0, 2, 1, 3).reshape(d, 2 * qk_w)

    c_ag = 4 * qk_w + 2 * D_ATTN
    half = COL_TILE
    ag = w[:, c_ag:c_ag + 2 * C_CONV].reshape(d, 2, C_CONV // half, half).transpose(0, 2, 1, 3).reshape(d, 2 * C_CONV)
    c_zc = c_ag + 2 * C_CONV
    parts = [ag, w[:, c_zc:c_zc + C_CONV], heads(0), heads(2 * qk_w), w[:, 4 * qk_w:c_ag], w[:, c_zc + C_CONV:]]
    w = jnp.concatenate(parts, axis=1).astype(_BF16)
    return w.reshape(d, N_COL_STEPS, COL_TILE).transpose(1, 0, 2)


def kernel(x_prompt, x_sample, cache_k, cache_v, state_conv, meta_tokens, rel_bias, norm_gain, w_in,
           q_norm_gain, k_norm_gain, lambda_qk, subln_gain, conv_w, conv_b, conv_ln_gain, conv_ln_bias,
           w_branch_out, w_out):
    n_b, seq, _ = x_prompt.shape
    n_s, t_s, _ = x_sample.shape
    past = cache_k.shape[3]
    assert w_in.shape[0] == 1 and t_s == N_META and seq % ATT_TILE == 0
    layer = 0
    lam_init = 0.8 - 0.6 * math.exp(-0.3 * layer)

    lq = lambda_qk[layer].astype(_F32)
    lam = (jnp.exp(jnp.sum(lq[0] * lq[1])) - jnp.exp(jnp.sum(lq[2] * lq[3])) + lam_init).reshape(1)

    w = _permute_w_in(w_in[layer])
    ng = norm_gain[layer].reshape(1, D_MODEL)
    qg = jnp.tile(q_norm_gain[layer], (1, 2))
    kg = jnp.tile(k_norm_gain[layer], (1, 2))
    cw = conv_w[layer]
    cb = conv_b[layer].reshape(1, C_CONV)
    lng = conv_ln_gain[layer].reshape(1, C_CONV)
    lnb = conv_ln_bias[layer].reshape(1, C_CONV)
    wb0 = w_branch_out[layer, 0].astype(_BF16)
    wb1 = w_branch_out[layer, 1].astype(_BF16)
    wo = w_out[layer].astype(_BF16)
    g_out = subln_gain[layer].astype(_F32) * (1.0 - lam_init)
    proj = functools.partial(_project, ng=ng, w=w, qg=qg, kg=kg, cw=cw, cb=cb, lng=lng, lnb=lnb)

    x_small = jnp.concatenate([meta_tokens.astype(_F32), x_sample.reshape(n_s * t_s, D_MODEL)], axis=0)
    halo_small = jnp.concatenate([jnp.zeros((1, HALO, C_CONV), _F32), state_conv[layer].astype(_F32)], axis=0)
    q_sm, k_sm, v_sm, za_sm, cvb_sm, gc_sm, ga_sm, tail_sm = proj(
        x_small, halo_small, n_seq=n_s + 1, seq_len=t_s, tm=(n_s + 1) * t_s)

    q_p, k_p, v_p, za_p, cvb_p, gc_p, ga_p, tail_p = proj(
        x_prompt.reshape(n_b * seq, D_MODEL), tail_sm[0:1], n_seq=n_b, seq_len=seq, tm=512,
        kv_front=(k_sm[0], v_sm[0]))

    T = ATT_TILE
    far = rel_bias[N_BUCKETS // 2 - 1].astype(_F32)
    r = jnp.arange(T)
    rel_d = r[:, None] - r[None, :]
    allowed = (r[:, None] // CHUNK) <= (r[None, :] // CHUNK)
    bd0 = jnp.where(allowed[None], jnp.transpose(_bucket_bias(rel_d, rel_bias) - far, (2, 0, 1)) * LOG2E, NEG)
    bd1 = jnp.transpose(_bucket_bias(rel_d - T, rel_bias) - far, (2, 0, 1)) * LOG2E
    rel_m = jnp.arange(N_META)[:, None] - (N_META + r[None, :])
    bm = jnp.transpose(_bucket_bias(rel_m, rel_bias) - far, (2, 0, 1)) * LOG2E
    g_t = jnp.broadcast_to(g_out[:, None], (D_VHEAD, T))

    at_p = _attention(lam, q_p, k_p, v_p, za_p.reshape(n_b, seq, D_ATTN), bd0, bd1, bm, g_t)

    q_pos = past + jnp.arange(t_s)
    bias_c = jnp.transpose(_bucket_bias(jnp.arange(past)[None, :] - q_pos[:, None], rel_bias), (2, 0, 1)) * LOG2E
    bias_n = jnp.transpose(_bucket_bias(q_pos[None, :] - q_pos[:, None], rel_bias), (2, 0, 1)) * LOG2E
    at_s = _decode_attention(lam, q_sm[1:], k_sm[1:], v_sm[1:], cache_k, cache_v,
                             za_sm[t_s:].reshape(n_s, t_s, D_ATTN), bias_c, bias_n, g_out.reshape(1, D_VHEAD))

    y_p = _finish(x_prompt.reshape(n_b * seq, D_MODEL), at_p.reshape(n_b * seq, D_ATTN), cvb_p, gc_p, ga_p,
                  wb0, wb1, wo, tm=512)
    y_s = _finish(x_sample.reshape(n_s * t_s, D_MODEL), at_s.reshape(n_s * t_s, D_ATTN), cvb_sm[t_s:],
                  gc_sm[t_s:], ga_sm[t_s:], wb0, wb1, wo, tm=n_s * t_s)

    return (y_p.reshape(n_b, seq, D_MODEL), y_s.reshape(n_s, t_s, D_MODEL), k_p[None], v_p[None],
            tail_p[None], k_sm[1:][None], v_sm[1:][None], tail_sm[1:][None])
```

```python
import functools
import math

import jax
import jax.numpy as jnp
import numpy as np
from jax import lax
from jax.experimental import pallas as pl
from jax.experimental.pallas import tpu as pltpu

D_MODEL = 2048
N_HEADS = 8
D_HEAD = 64
D_VHEAD = 2 * D_HEAD
D_ATTN = N_HEADS * D_VHEAD
C_CONV = D_MODEL // 2
CONV_WIDTH = 31
HALO = CONV_WIDTH - 1
N_META = 16
CHUNK = 64
N_BUCKETS = 32
MAX_DISTANCE = 128
EPS = 1e-6
NEG = -1e30
SCALE = D_HEAD ** -0.5
LOG2E = math.log2(math.e)
Q_SCALE = SCALE * LOG2E

COL_TILE = 512
N_COL_STEPS = 22
STEP_GLU, STEP_ZC, STEP_Q, STEP_K, STEP_V, STEP_ZA, STEP_GC, STEP_GA, STEP_END = 0, 4, 6, 8, 10, 12, 14, 18, 22
N_W_RESIDENT = 6
N_W_SLOTS = 4
assert N_W_RESIDENT >= 1 and (N_COL_STEPS - N_W_RESIDENT - N_W_SLOTS) % 2 == 0
HALO_PAD = 32
CONV_ROWS = 32
ATT_TILE = 256
VMEM_LIMIT = 56 * 1024 * 1024
VMEM_LIMIT_PROJ = 60 * 1024 * 1024

_BF16 = jnp.bfloat16
_F32 = jnp.float32


def _sigmoid(x):
    return 1.0 / (1.0 + jnp.exp(-x))


def _silu(x):
    return x * _sigmoid(x)


def _head_norm(u, gain_ref, h0, scale):
    lane = lax.broadcasted_iota(jnp.int32, (1, D_VHEAD), 1)
    lo = lane < D_HEAD
    outs = []
    for hl in range(u.shape[1] // D_VHEAD):
        xh = u[:, hl * D_VHEAD:(hl + 1) * D_VHEAD]
        sq = xh * xh
        s_lo = jnp.sum(jnp.where(lo, sq, 0.0), axis=-1, keepdims=True)
        s_hi = jnp.sum(jnp.where(lo, 0.0, sq), axis=-1, keepdims=True)
        r = jnp.where(lo, lax.rsqrt(s_lo * (1.0 / D_HEAD) + EPS), lax.rsqrt(s_hi * (1.0 / D_HEAD) + EPS))
        y = xh * r * gain_ref[h0 + hl:h0 + hl + 1, :]
        if scale != 1.0:
            y = y * scale
        outs.append(y)
    return outs


def _proj_kernel(x_ref, halo_ref, ng_ref, w_hbm, qg_ref, kg_ref, cw_ref, cb_ref, lng_ref, lnb_ref, *rest,
                 ns, tt, tiles_per_seq, kv_row_offset):
    kv_manual = kv_row_offset > 0
    if kv_manual:
        (km_ref, vm_ref, q_ref, k_hbm, v_hbm, za_ref, cvb_ref, gc_ref, ga_ref, tail_ref,
         h_ref, *w_buf, u_ref0, u_ref1, g_ref, zc_ref, y_ref, sem, k_stage, v_stage, kv_sem) = rest
    else:
        (q_ref, k_ref, v_ref, za_ref, cvb_ref, gc_ref, ga_ref, tail_ref,
         h_ref, *w_buf, u_ref0, u_ref1, g_ref, zc_ref, y_ref, sem) = rest
    assert len(w_buf) == N_W_RESIDENT + N_W_SLOTS
    i = pl.program_id(0)
    n_i = pl.num_programs(0)
    seq_rows = HALO_PAD + tt
    hpt = COL_TILE // D_VHEAD
    lpt = COL_TILE // 128
    n_lane_tiles = C_CONV // 128
    rb = min(tt, CONV_ROWS)

    u_ref = (u_ref0, u_ref1)

    def w_index(step):
        return step if step < N_W_RESIDENT else N_W_RESIDENT + (step - N_W_RESIDENT) % N_W_SLOTS

    def w_copy(step):
        return pltpu.make_async_copy(w_hbm.at[step], w_buf[w_index(step)], sem.at[w_index(step)])

    @pl.when(i == 0)
    def _():
        for t in range(N_W_RESIDENT + N_W_SLOTS):
            w_copy(t).start()
        for t in range(N_W_RESIDENT):
            w_copy(t).wait()

    if kv_manual:
        assert ns == 1
        seq_idx = i // tiles_per_seq
        row0 = pl.multiple_of(kv_row_offset + (i % tiles_per_seq) * tt, 8)

        def kv_copy(which):
            if which < 2:
                src, dst = (k_stage, k_hbm) if which == 0 else (v_stage, v_hbm)
                dst = dst.at[seq_idx, :, pl.ds(row0, tt), :]
            else:
                src, dst = (km_ref, k_hbm) if which == 2 else (vm_ref, v_hbm)
                dst = dst.at[seq_idx, :, pl.ds(0, kv_row_offset), :]
            return pltpu.make_async_copy(src, dst, kv_sem.at[which])

        @pl.when(i % tiles_per_seq == 0)
        def _():
            kv_copy(2).start()
            kv_copy(3).start()

    if tiles_per_seq == 1:
        for s in range(ns):
            for c in range(n_lane_tiles):
                g_ref[c, s * seq_rows + 2:s * seq_rows + HALO_PAD, :] = halo_ref[s, :, c * 128:(c + 1) * 128]
    else:
        @pl.when(i % tiles_per_seq == 0)
        def _():
            for c in range(n_lane_tiles):
                g_ref[c, 2:HALO_PAD, :] = halo_ref[0, :, c * 128:(c + 1) * 128]

    x = x_ref[...]
    ms = jnp.mean(x * x, axis=-1, keepdims=True)
    h_ref[...] = (x * lax.rsqrt(ms + EPS) * ng_ref[...]).astype(_BF16)

    def split(n, parts, align):
        cuts = [align * round(n * k / (parts * align)) for k in range(parts + 1)]
        cuts[0], cuts[-1] = 0, n
        return list(zip(cuts[:-1], cuts[1:]))

    tm = ns * tt
    mm_pieces = [(rs, cs) for rs in split(tm, 2, 16) for cs in split(COL_TILE, 2, 256)]
    row_quarters = split(tm, 8, max(tt if ns > 1 else 16, 16))

    def matmul_piece(step, piece):
        (r0, r1), (c0, c1) = piece
        u_ref[step % 2][r0:r1, c0:c1] = jnp.dot(h_ref[r0:r1, :], w_buf[w_index(step)][:, c0:c1],
                                                preferred_element_type=_F32)

    def seq_pieces(r0, r1):
        out = []
        for s in range(ns):
            lo, hi = max(r0, s * tt), min(r1, (s + 1) * tt)
            if lo < hi:
                out.append((s, lo - s * tt, hi - s * tt))
        return out

    def epilogue(step, r0, r1):
        if r0 == r1:
            return
        u = u_ref[step % 2][r0:r1, :]
        pieces = seq_pieces(r0, r1)

        def rows(s, lo, hi):
            return slice(s * tt + lo - r0, s * tt + hi - r0)

        if STEP_Q <= step < STEP_V:
            is_q = step < STEP_K
            h0 = (step - (STEP_Q if is_q else STEP_K)) * hpt
            ys = _head_norm(u, qg_ref if is_q else kg_ref, h0, Q_SCALE if is_q else 1.0)
            for hl in range(hpt):
                for (s, lo, hi) in pieces:
                    if is_q:
                        q_ref[s, h0 + hl, lo:hi, :] = ys[hl][rows(s, lo, hi)].astype(_BF16)
                    elif kv_manual:
                        k_stage[h0 + hl, lo:hi, :] = ys[hl][rows(s, lo, hi)]
                    else:
                        k_ref[s, h0 + hl, lo:hi, :] = ys[hl][rows(s, lo, hi)]
        elif STEP_V <= step < STEP_ZA:
            h0 = (step - STEP_V) * hpt
            for hl in range(hpt):
                for (s, lo, hi) in pieces:
                    blk = u[rows(s, lo, hi), hl * D_VHEAD:(hl + 1) * D_VHEAD]
                    if kv_manual:
                        v_stage[h0 + hl, lo:hi, :] = blk
                    else:
                        v_ref[s, h0 + hl, lo:hi, :] = blk
        elif STEP_ZA <= step < STEP_GC:
            c0 = (step - STEP_ZA) * COL_TILE
            za_ref[r0:r1, c0:c0 + COL_TILE] = _silu(u).astype(_BF16)
        elif STEP_GLU <= step < STEP_ZC:
            k = step - STEP_GLU
            lt0 = (k // 2) * lpt
            gate = _sigmoid(u) if k % 2 else None
            for (s, lo, hi) in pieces:
                dst = slice(s * seq_rows + HALO_PAD + lo, s * seq_rows + HALO_PAD + hi)
                for c in range(lpt):
                    blk = u[rows(s, lo, hi), c * 128:(c + 1) * 128]
                    if gate is None:
                        g_ref[lt0 + c, dst, :] = blk
                    else:
                        g_ref[lt0 + c, dst, :] = g_ref[lt0 + c, dst, :] * gate[rows(s, lo, hi), c * 128:(c + 1) * 128]
        elif STEP_ZC <= step < STEP_Q:
            c0 = (step - STEP_ZC) * COL_TILE
            zc_ref[r0:r1, c0:c0 + COL_TILE] = _silu(u)
        elif STEP_GC <= step < STEP_GA:
            c0 = (step - STEP_GC) * COL_TILE
            gc_ref[r0:r1, c0:c0 + COL_TILE] = _sigmoid(u).astype(_BF16)
        else:
            assert STEP_GA <= step < STEP_END
            c0 = (step - STEP_GA) * COL_TILE
            ga_ref[r0:r1, c0:c0 + COL_TILE] = _sigmoid(u).astype(_BF16)

    def conv_unit(s, r0, c):
        cs = slice(c * 128, (c + 1) * 128)
        base = s * seq_rows + r0
        acc = jnp.broadcast_to(cb_ref[:, cs], (rb, 128))
        for r in range(8):
            taps = [(a, 8 * a + r - 2) for a in range(HALO_PAD // 8 + 1) if 0 <= 8 * a + r - 2 < CONV_WIDTH]
            phase = g_ref[c, base + r:base + r + 8 * taps[-1][0] + rb, :]
            for a, t in taps:
                acc = acc + cw_ref[t:t + 1, cs] * phase[8 * a:8 * a + rb]
        y_ref[s * tt + r0:s * tt + r0 + rb, cs] = acc

    def norm_unit(s, r0):
        rows = slice(s * tt + r0, s * tt + r0 + rb)
        ys = [y_ref[rows, c * 128:(c + 1) * 128] for c in range(n_lane_tiles)]
        tot = ys[0]
        for c in range(1, n_lane_tiles):
            tot = tot + ys[c]
        mu = jnp.sum(tot, axis=-1, keepdims=True) * (1.0 / C_CONV)
        cen = [a - mu for a in ys]
        sq = cen[0] * cen[0]
        for c in range(1, n_lane_tiles):
            sq = sq + cen[c] * cen[c]
        rstd = lax.rsqrt(jnp.sum(sq, axis=-1, keepdims=True) * (1.0 / C_CONV) + EPS)
        for c in range(n_lane_tiles):
            cs = slice(c * 128, (c + 1) * 128)
            y = _silu(cen[c] * rstd * lng_ref[:, cs] + lnb_ref[:, cs])
            cvb_ref[rows, cs] = (y * zc_ref[rows, cs]).astype(_BF16)

    blocks = [(s, r0) for s in range(ns) for r0 in range(0, tt, rb)]
    vregs = rb // 8
    item_cost = {"conv": (2 * CONV_WIDTH + 2) * vregs, "norm": 18 * vregs * n_lane_tiles}
    norm_lag = 2

    def epilogue_cost(step):
        per_vreg = (11 if STEP_Q <= step < STEP_V else 0 if STEP_V <= step < STEP_ZA
                    else 5 if step % 2 or step >= STEP_ZC else 0)
        return per_vreg * tm * COL_TILE // 1024

    items = []
    for bi in range(len(blocks) + norm_lag):
        if bi < len(blocks):
            items += [("conv",) + blocks[bi] + (c,) for c in range(n_lane_tiles)]
        if bi >= norm_lag:
            items.append(("norm",) + blocks[bi - norm_lag])
    work_steps = list(range(STEP_ZC, N_COL_STEPS))
    target = (sum(item_cost[it[0]] for it in items) + sum(map(epilogue_cost, work_steps))) / len(work_steps)
    work_at, pos = {}, 0
    for st in work_steps:
        budget, first = target - epilogue_cost(st), pos
        while pos < len(items) and (budget > 0 or st == work_steps[-1]):
            if items[pos][0] == "norm" and st < STEP_Q:
                break
            budget -= item_cost[items[pos][0]]
            pos += 1
        work_at[st] = items[first:pos]
    assert pos == len(items)

    for piece in mm_pieces:
        matmul_piece(0, piece)
    n_q = len(row_quarters)

    def vpu_chunk(step, k):
        work = work_at.get(step, [])
        epilogue(step, *row_quarters[k])
        for it in work[k * len(work) // n_q:(k + 1) * len(work) // n_q]:
            if it[0] == "conv":
                conv_unit(*it[1:])
            else:
                norm_unit(*it[1:])
        if kv_manual and k == n_q - 1 and step in (STEP_V - 1, STEP_ZA - 1):
            kv_copy(0 if step == STEP_V - 1 else 1).start()

    carried = None
    for step in range(N_COL_STEPS):
        if step == N_W_RESIDENT - 1:
            w_copy(step + 1).wait()
            w_copy(step + 2).wait()
        elif step > N_W_RESIDENT and (step - N_W_RESIDENT) % 2 == 1:
            for t in (step + 1, step + 2):
                if t < N_COL_STEPS:
                    w_copy(t).wait()
            for t in (step + 3, step + 4):
                if t < N_COL_STEPS:
                    w_copy(t).start()
        if carried is not None:
            vpu_chunk(*carried)
        per_piece = n_q // len(mm_pieces)
        for p, piece in enumerate(mm_pieces):
            if step + 1 < N_COL_STEPS:
                matmul_piece(step + 1, piece)
            for k in range(p * per_piece, (p + 1) * per_piece):
                if step + 1 < N_COL_STEPS and k == n_q - 1:
                    carried = (step, k)
                else:
                    vpu_chunk(step, k)

    if tiles_per_seq == 1:
        for s in range(ns):
            for c in range(n_lane_tiles):
                tail_ref[s, :, c * 128:(c + 1) * 128] = g_ref[c, (s + 1) * seq_rows - HALO:(s + 1) * seq_rows, :]
    else:
        @pl.when(i % tiles_per_seq == tiles_per_seq - 1)
        def _():
            for c in range(n_lane_tiles):
                tail_ref[0, :, c * 128:(c + 1) * 128] = g_ref[c, seq_rows - HALO:seq_rows, :]
        g_ref[:, 0:HALO_PAD, :] = g_ref[:, tt:tt + HALO_PAD, :]

    if kv_manual:
        kv_copy(0).wait()
        kv_copy(1).wait()

        @pl.when(i % tiles_per_seq == 0)
        def _():
            kv_copy(2).wait()
            kv_copy(3).wait()

    @pl.when(i + 1 < n_i)
    def _():
        for t in range(N_W_RESIDENT, N_W_RESIDENT + N_W_SLOTS):
            w_copy(t).start()


def _project(x, halo, ng, w, qg, kg, cw, cb, lng, lnb, *, n_seq, seq_len, tm, kv_front=None):
    kv_row_offset = kv_front[0].shape[1] if kv_front is not None else 0
    rows = n_seq * seq_len
    if seq_len >= tm:
        assert seq_len % tm == 0
        ns, tt, tps = 1, tm, seq_len // tm
    else:
        assert tm % seq_len == 0 and rows % tm == 0
        ns, tt, tps = tm // seq_len, seq_len, 1
    n_tiles = rows // tm
    halo_bcast = halo.shape[0] == 1

    def seq_map(i):
        return (i // tps, 0, i % tps, 0) if tps > 1 else (i, 0, 0, 0)

    def halo_map(i):
        if halo_bcast:
            return (0, 0, 0)
        return (i // tps, 0, 0) if tps > 1 else (i, 0, 0)

    def tail_map(i):
        return (i // tps, 0, 0) if tps > 1 else (i, 0, 0)

    const2 = lambda i: (0, 0)
    head_blk = (ns, N_HEADS, tt, D_VHEAD)
    if kv_row_offset:
        assert ns == 1
        kv_spec = pl.BlockSpec(memory_space=pl.ANY)
    else:
        kv_spec = pl.BlockSpec(head_blk, seq_map)
    kv_rows = kv_row_offset + seq_len
    row = lambda i: (i, 0)
    kern = functools.partial(_proj_kernel, ns=ns, tt=tt, tiles_per_seq=tps, kv_row_offset=kv_row_offset)
    out_shape = (
        jax.ShapeDtypeStruct((n_seq, N_HEADS, seq_len, D_VHEAD), _BF16),
        jax.ShapeDtypeStruct((n_seq, N_HEADS, kv_rows, D_VHEAD), _F32),
        jax.ShapeDtypeStruct((n_seq, N_HEADS, kv_rows, D_VHEAD), _F32),
        jax.ShapeDtypeStruct((rows, D_ATTN), _BF16),
        jax.ShapeDtypeStruct((rows, C_CONV), _BF16),
        jax.ShapeDtypeStruct((rows, D_MODEL), _BF16),
        jax.ShapeDtypeStruct((rows, D_MODEL), _BF16),
        jax.ShapeDtypeStruct((n_seq, HALO, C_CONV), _F32),
    )
    in_specs = [
        pl.BlockSpec((tm, D_MODEL), row),
        pl.BlockSpec((1 if halo_bcast or tps > 1 else ns, HALO, C_CONV), halo_map),
        pl.BlockSpec((1, D_MODEL), const2),
        pl.BlockSpec(memory_space=pl.ANY),
        pl.BlockSpec((N_HEADS, D_VHEAD), const2),
        pl.BlockSpec((N_HEADS, D_VHEAD), const2),
        pl.BlockSpec((CONV_WIDTH, C_CONV), const2),
        pl.BlockSpec((1, C_CONV), const2),
        pl.BlockSpec((1, C_CONV), const2),
        pl.BlockSpec((1, C_CONV), const2),
    ]
    operands = [x, halo, ng, w, qg, kg, cw, cb, lng, lnb]
    if kv_row_offset:
        front = pl.BlockSpec((N_HEADS, kv_row_offset, D_VHEAD), lambda i: (0, 0, 0))
        in_specs += [front, front]
        operands += list(kv_front)
    out_specs = (
        pl.BlockSpec(head_blk, seq_map),
        kv_spec,
        kv_spec,
        pl.BlockSpec((tm, D_ATTN), row),
        pl.BlockSpec((tm, C_CONV), row),
        pl.BlockSpec((tm, D_MODEL), row),
        pl.BlockSpec((tm, D_MODEL), row),
        pl.BlockSpec((1 if tps > 1 else ns, HALO, C_CONV), tail_map),
    )
    scratch = [
        pltpu.VMEM((tm, D_MODEL), _BF16),
    ] + [pltpu.VMEM((D_MODEL, COL_TILE), _BF16) for _ in range(N_W_RESIDENT + N_W_SLOTS)] + [
        pltpu.VMEM((tm, COL_TILE), _F32),
        pltpu.VMEM((tm, COL_TILE), _F32),
        pltpu.VMEM((C_CONV // 128, ns * (HALO_PAD + tt), 128), _F32),
        pltpu.VMEM((tm, C_CONV), _F32),
        pltpu.VMEM((tm, C_CONV), _F32),
        pltpu.SemaphoreType.DMA((N_W_RESIDENT + N_W_SLOTS,)),
    ]
    if kv_row_offset:
        scratch += [
            pltpu.VMEM((N_HEADS, tt, D_VHEAD), _F32),
            pltpu.VMEM((N_HEADS, tt, D_VHEAD), _F32),
            pltpu.SemaphoreType.DMA((4,)),
        ]
    return pl.pallas_call(
        kern, grid=(n_tiles,), in_specs=in_specs, out_specs=out_specs, out_shape=out_shape,
        scratch_shapes=scratch,
        compiler_params=pltpu.CompilerParams(
            dimension_semantics=("arbitrary",), vmem_limit_bytes=VMEM_LIMIT_PROJ),
        name="proj",
    )(*operands)


def _attn_kernel(lam_ref, q_ref, k_ref, v_ref, za_ref, bd0_ref, bd1_ref, bm_ref, g_ref,
                 o_ref, kb_ref, vt_ref, s_ref, p_ref, *, seq_len):
    T = ATT_TILE
    n_q = seq_len // T
    lam = lam_ref[0]
    kb_ref[...] = k_ref[0, 0, N_META:, :].astype(_BF16)
    vt_ref[...] = v_ref[0, 0, N_META:, :].T.astype(_BF16)
    kmb = k_ref[0, 0, 0:N_META, :].astype(_BF16)
    vmt = v_ref[0, 0, 0:N_META, :].T.astype(_BF16)
    lane = lax.broadcasted_iota(jnp.int32, (1, D_VHEAD), 1)
    lo = lane < D_HEAD
    nt = (((1,), (1,)), ((), ()))

    def col_reduce(x, op):
        return op(x.reshape(x.shape[0] // 8, 8, T), axis=0)

    items = [(i, mp) for i in range(n_q) for mp in range(2)]
    qs_cache, m8, sm, mx, l8, outs = {}, {}, {}, {}, {}, {}

    def q_of(w):
        i, mp = items[w]
        if i not in qs_cache:
            q = q_ref[0, 0, i * T:(i + 1) * T, :]
            zero = jnp.zeros_like(q)
            qs_cache[i] = (jnp.where(lo, q, zero), jnp.where(lo, zero, q))
        return qs_cache[i][mp]

    def score_tile(w, j):
        i, _ = items[w]
        s = lax.dot_general(kb_ref[j * T:(j + 1) * T, :], q_of(w), nt, preferred_element_type=_F32)
        if j == i:
            s = s + bd0_ref[0]
        elif j == i - 1:
            s = s + bd1_ref[0]
        s_ref[w % 2, j * T:(j + 1) * T, :] = s
        t8 = col_reduce(s, jnp.max)
        m8[w] = t8 if j == 0 else jnp.maximum(m8[w], t8)

    def score_finish(w):
        i, _ = items[w]
        s = lax.dot_general(kmb, q_of(w), nt, preferred_element_type=_F32)
        if i == 0:
            s = s + bm_ref[0]
        sm[w] = s
        mx[w] = jnp.maximum(jnp.max(m8[w], axis=0, keepdims=True), jnp.max(s, axis=0, keepdims=True))

    def exp_tile(w, j):
        p = jnp.exp2(s_ref[w % 2, j * T:(j + 1) * T, :] - mx[w])
        p_ref[w % 2, j * T:(j + 1) * T, :] = p.astype(_BF16)
        t8 = col_reduce(p, jnp.sum)
        l8[w] = t8 if j == 0 else l8[w] + t8

    def value_matmul(w):
        i, _ = items[w]
        nk = (i + 1) * T
        pm = jnp.exp2(sm[w] - mx[w])
        l = jnp.sum(l8[w], axis=0, keepdims=True) + jnp.sum(pm, axis=0, keepdims=True)
        acc = (jnp.dot(vt_ref[:, 0:nk], p_ref[w % 2, 0:nk, :], preferred_element_type=_F32)
               + jnp.dot(vmt, pm.astype(_BF16), preferred_element_type=_F32))
        outs[w] = acc / l

    def finalize(i):
        o = outs[2 * i] - lam * outs[2 * i + 1]
        ms = jnp.mean(o * o, axis=0, keepdims=True)
        y = o * lax.rsqrt(ms + EPS) * g_ref[...]
        yt = y.T
        o_ref[0, i * T:(i + 1) * T, :] = (yt * za_ref[0, i * T:(i + 1) * T, :].astype(_F32)).astype(_BF16)

    n_items = len(items)
    for j in range(items[0][0] + 1):
        score_tile(0, j)
    score_finish(0)
    for w in range(n_items):
        n_exp = items[w][0] + 1
        n_next = items[w + 1][0] + 1 if w + 1 < n_items else 0
        for j in range(max(n_exp, n_next)):
            if j < n_next:
                score_tile(w + 1, j)
            if j < n_exp:
                exp_tile(w, j)
        if n_next:
            score_finish(w + 1)
        value_matmul(w)
        if items[w][1] == 1:
            finalize(items[w][0])


def _attention(lam, q, k, v, za, bd0, bd1, bm, g):
    n_b, _, seq_len, _ = q.shape
    T = ATT_TILE
    assert seq_len % T == 0
    kern = functools.partial(_attn_kernel, seq_len=seq_len)
    bh = lambda b, h: (b, h, 0, 0)
    hd = lambda b, h: (h, 0, 0)
    in_specs = [
        pl.BlockSpec(memory_space=pltpu.SMEM),
        pl.BlockSpec((1, 1, seq_len, D_VHEAD), bh),
        pl.BlockSpec((1, 1, N_META + seq_len, D_VHEAD), bh),
        pl.BlockSpec((1, 1, N_META + seq_len, D_VHEAD), bh),
        pl.BlockSpec((1, seq_len, D_VHEAD), lambda b, h: (b, 0, h)),
        pl.BlockSpec((1, T, T), hd),
        pl.BlockSpec((1, T, T), hd),
        pl.BlockSpec((1, N_META, T), hd),
        pl.BlockSpec((D_VHEAD, T), lambda b, h: (0, 0)),
    ]
    return pl.pallas_call(
        kern, grid=(n_b, N_HEADS), in_specs=in_specs,
        out_specs=pl.BlockSpec((1, seq_len, D_VHEAD), lambda b, h: (b, 0, h)),
        out_shape=jax.ShapeDtypeStruct((n_b, seq_len, D_ATTN), _BF16),
        scratch_shapes=[
            pltpu.VMEM((seq_len, D_VHEAD), _BF16),
            pltpu.VMEM((D_VHEAD, seq_len), _BF16),
            pltpu.VMEM((2, seq_len, T), _F32),
            pltpu.VMEM((2, seq_len, T), _BF16),
        ],
        compiler_params=pltpu.CompilerParams(
            dimension_semantics=("arbitrary", "arbitrary"), vmem_limit_bytes=VMEM_LIMIT),
        name="attn",
    )(lam, q, k, v, za, bd0, bd1, bm, g)


def _decode_kernel(lam_ref, q_ref, kn_ref, vn_ref, ck_ref, cv_ref, za_ref, bc_ref, bn_ref, g_ref, o_ref):
    lam = lam_ref[0]
    lane = lax.broadcasted_iota(jnp.int32, (1, D_VHEAD), 1)
    lo = lane < D_HEAD
    nt = (((1,), (1,)), ((), ()))
    scores, weights = {}, {}

    def score_stage(h):
        q = q_ref[0, h]
        zero = jnp.zeros_like(q)
        qs = (jnp.where(lo, q, zero), jnp.where(lo, zero, q))
        ck = ck_ref[0, 0, h].astype(_BF16)
        kn = kn_ref[0, h].astype(_BF16)
        scores[h] = [(lax.dot_general(qs[mp], ck, nt, preferred_element_type=_F32) + bc_ref[h],
                      lax.dot_general(qs[mp], kn, nt, preferred_element_type=_F32) + bn_ref[h])
                     for mp in range(2)]

    def softmax_stage(h):
        ps = []
        for sc, sn in scores.pop(h):
            m = jnp.maximum(jnp.max(sc, axis=-1, keepdims=True), jnp.max(sn, axis=-1, keepdims=True))
            ec = jnp.exp2(sc - m)
            en = jnp.exp2(sn - m)
            l = jnp.sum(ec, axis=-1, keepdims=True) + jnp.sum(en, axis=-1, keepdims=True)
            ps.append((ec / l, en / l))
        weights[h] = ((ps[0][0] - lam * ps[1][0]).astype(_BF16), (ps[0][1] - lam * ps[1][1]).astype(_BF16))

    def value_stage(h):
        ac, an = weights.pop(h)
        o = (jnp.dot(ac, cv_ref[0, 0, h].astype(_BF16), preferred_element_type=_F32)
             + jnp.dot(an, vn_ref[0, h].astype(_BF16), preferred_element_type=_F32))
        ms = jnp.mean(o * o, axis=-1, keepdims=True)
        y = o * lax.rsqrt(ms + EPS) * g_ref[...]
        cs = slice(h * D_VHEAD, (h + 1) * D_VHEAD)
        o_ref[0, :, cs] = (y * za_ref[0, :, cs].astype(_F32)).astype(_BF16)

    for t in range(N_HEADS + 2):
        if t < N_HEADS:
            score_stage(t)
        if 0 <= t - 1 < N_HEADS:
            softmax_stage(t - 1)
        if 0 <= t - 2 < N_HEADS:
            value_stage(t - 2)


def _decode_attention(lam, q, kn, vn, ck, cv, za, bc, bn, g):
    n_b, _, t_s, _ = q.shape
    past = ck.shape[3]
    b4 = lambda b: (b, 0, 0, 0)
    c3 = lambda b: (0, 0, 0)
    in_specs = [
        pl.BlockSpec(memory_space=pltpu.SMEM),
        pl.BlockSpec((1, N_HEADS, t_s, D_VHEAD), b4),
        pl.BlockSpec((1, N_HEADS, t_s, D_VHEAD), b4),
        pl.BlockSpec((1, N_HEADS, t_s, D_VHEAD), b4),
        pl.BlockSpec((1, 1, N_HEADS, past, D_VHEAD), lambda b: (0, b, 0, 0, 0)),
        pl.BlockSpec((1, 1, N_HEADS, past, D_VHEAD), lambda b: (0, b, 0, 0, 0)),
        pl.BlockSpec((1, t_s, D_ATTN), lambda b: (b, 0, 0)),
        pl.BlockSpec((N_HEADS, t_s, past), c3),
        pl.BlockSpec((N_HEADS, t_s, t_s), c3),
        pl.BlockSpec((1, D_VHEAD), lambda b: (0, 0)),
    ]
    return pl.pallas_call(
        _decode_kernel, grid=(n_b,), in_specs=in_specs,
        out_specs=pl.BlockSpec((1, t_s, D_ATTN), lambda b: (b, 0, 0)),
        out_shape=jax.ShapeDtypeStruct((n_b, t_s, D_ATTN), _BF16),
        compiler_params=pltpu.CompilerParams(
            dimension_semantics=("arbitrary",), vmem_limit_bytes=VMEM_LIMIT),
        name="decode_attn",
    )(lam, q, kn, vn, ck, cv, za, bc, bn, g)


def _finish_kernel(x_ref, at_ref, cv_ref, gc_ref, ga_ref, wb0_ref, wb1_ref, wo_ref, y_ref):
    pc = jnp.dot(cv_ref[...], wb0_ref[...], preferred_element_type=_F32)
    pa = jnp.dot(at_ref[...], wb1_ref[...], preferred_element_type=_F32)
    merged = gc_ref[...].astype(_F32) * pc + ga_ref[...].astype(_F32) * pa
    y_ref[...] = x_ref[...] + jnp.dot(merged.astype(_BF16), wo_ref[...], preferred_element_type=_F32)


def _finish(x, at, cv, gc, ga, wb0, wb1, wo, *, tm):
    rows = x.shape[0]
    assert rows % tm == 0
    row = lambda i: (i, 0)
    const = lambda i: (0, 0)
    single = pl.Buffered(1)
    in_specs = [
        pl.BlockSpec((tm, D_MODEL), row),
        pl.BlockSpec((tm, D_ATTN), row),
        pl.BlockSpec((tm, C_CONV), row),
        pl.BlockSpec((tm, D_MODEL), row),
        pl.BlockSpec((tm, D_MODEL), row),
        pl.BlockSpec((C_CONV, D_MODEL), const, pipeline_mode=single),
        pl.BlockSpec((D_ATTN, D_MODEL), const, pipeline_mode=single),
        pl.BlockSpec((D_MODEL, D_MODEL), const, pipeline_mode=single),
    ]
    return pl.pallas_call(
        _finish_kernel, grid=(rows // tm,), in_specs=in_specs,
        out_specs=pl.BlockSpec((tm, D_MODEL), row),
        out_shape=jax.ShapeDtypeStruct((rows, D_MODEL), _F32),
        compiler_params=pltpu.CompilerParams(
            dimension_semantics=("arbitrary",), vmem_limit_bytes=VMEM_LIMIT),
        name="finish",
    )(x, at, cv, gc, ga, wb0, wb1, wo)


def _bucket_bias(rel, table):
    half = N_BUCKETS // 2
    max_exact = half // 2
    n = jnp.abs(rel)
    n_f = jnp.maximum(n, 1).astype(jnp.float32)
    large = max_exact + (jnp.log(n_f / max_exact) / math.log(MAX_DISTANCE / max_exact)
                         * (half - max_exact)).astype(jnp.int32)
    large = jnp.minimum(large, half - 1)
    bucket = jnp.where(rel > 0, half, 0) + jnp.where(n < max_exact, n, large)
    onehot = (bucket[..., None] == jnp.arange(N_BUCKETS)).astype(jnp.float32)
    return jnp.einsum('...b,bh->...h', onehot, table.astype(jnp.float32), precision=lax.Precision.HIGHEST)


def _permute_w_in(w):
    qk_w = N_HEADS * D_HEAD
    d = w.shape[0]

    def heads(c0):
        return w[:, c0:c0 + 2 * qk_w].reshape(d, 2, N_HEADS, D_HEAD).transpose(0, 2, 1, 3).reshape(d, 2 * qk_w)

    c_ag = 4 * qk_w + 2 * D_ATTN
    half = COL_TILE
    ag = w[:, c_ag:c_ag + 2 * C_CONV].reshape(d, 2, C_CONV // half, half).transpose(0, 2, 1, 3).reshape(d, 2 * C_CONV)
    c_zc = c_ag + 2 * C_CONV
    parts = [ag, w[:, c_zc:c_zc + C_CONV], heads(0), heads(2 * qk_w), w[:, 4 * qk_w:c_ag], w[:, c_zc + C_CONV:]]
    w = jnp.concatenate(parts, axis=1).astype(_BF16)
    return w.reshape(d, N_COL_STEPS, COL_TILE).transpose(1, 0, 2)


def kernel(x_prompt, x_sample, cache_k, cache_v, state_conv, meta_tokens, rel_bias, norm_gain, w_in,
           q_norm_gain, k_norm_gain, lambda_qk, subln_gain, conv_w, conv_b, conv_ln_gain, conv_ln_bias,
           w_branch_out, w_out):
    n_b, seq, _ = x_prompt.shape
    n_s, t_s, _ = x_sample.shape
    past = cache_k.shape[3]
    assert w_in.shape[0] == 1 and t_s == N_META and seq % ATT_TILE == 0
    layer = 0
    lam_init = 0.8 - 0.6 * math.exp(-0.3 * layer)

    lq = lambda_qk[layer].astype(_F32)
    lam = (jnp.exp(jnp.sum(lq[0] * lq[1])) - jnp.exp(jnp.sum(lq[2] * lq[3])) + lam_init).reshape(1)

    w = _permute_w_in(w_in[layer])
    ng = norm_gain[layer].reshape(1, D_MODEL)
    qg = jnp.tile(q_norm_gain[layer], (1, 2))
    kg = jnp.tile(k_norm_gain[layer], (1, 2))
    cw = conv_w[layer]
    cb = conv_b[layer].reshape(1, C_CONV)
    lng = conv_ln_gain[layer].reshape(1, C_CONV)
    lnb = conv_ln_bias[layer].reshape(1, C_CONV)
    wb0 = w_branch_out[layer, 0].astype(_BF16)
    wb1 = w_branch_out[layer, 1].astype(_BF16)
    wo = w_out[layer].astype(_BF16)
    g_out = subln_gain[layer].astype(_F32) * (1.0 - lam_init)
    proj = functools.partial(_project, ng=ng, w=w, qg=qg, kg=kg, cw=cw, cb=cb, lng=lng, lnb=lnb)

    x_small = jnp.concatenate([meta_tokens.astype(_F32), x_sample.reshape(n_s * t_s, D_MODEL)], axis=0)
    halo_small = jnp.concatenate([jnp.zeros((1, HALO, C_CONV), _F32), state_conv[layer].astype(_F32)], axis=0)
    q_sm, k_sm, v_sm, za_sm, cvb_sm, gc_sm, ga_sm, tail_sm = proj(
        x_small, halo_small, n_seq=n_s + 1, seq_len=t_s, tm=(n_s + 1) * t_s)

    q_p, k_p, v_p, za_p, cvb_p, gc_p, ga_p, tail_p = proj(
        x_prompt.reshape(n_b * seq, D_MODEL), tail_sm[0:1], n_seq=n_b, seq_len=seq, tm=512,
        kv_front=(k_sm[0], v_sm[0]))

    T = ATT_TILE
    far = rel_bias[N_BUCKETS // 2 - 1].astype(_F32)
    r = jnp.arange(T)
    rel_d = r[:, None] - r[None, :]
    allowed = (r[:, None] // CHUNK) <= (r[None, :] // CHUNK)
    bd0 = jnp.where(allowed[None], jnp.transpose(_bucket_bias(rel_d, rel_bias) - far, (2, 0, 1)) * LOG2E, NEG)
    bd1 = jnp.transpose(_bucket_bias(rel_d - T, rel_bias) - far, (2, 0, 1)) * LOG2E
    rel_m = jnp.arange(N_META)[:, None] - (N_META + r[None, :])
    bm = jnp.transpose(_bucket_bias(rel_m, rel_bias) - far, (2, 0, 1)) * LOG2E
    g_t = jnp.broadcast_to(g_out[:, None], (D_VHEAD, T))

    at_p = _attention(lam, q_p, k_p, v_p, za_p.reshape(n_b, seq, D_ATTN), bd0, bd1, bm, g_t)

    q_pos = past + jnp.arange(t_s)
    bias_c = jnp.transpose(_bucket_bias(jnp.arange(past)[None, :] - q_pos[:, None], rel_bias), (2, 0, 1)) * LOG2E
    bias_n = jnp.transpose(_bucket_bias(q_pos[None, :] - q_pos[:, None], rel_bias), (2, 0, 1)) * LOG2E
    at_s = _decode_attention(lam, q_sm[1:], k_sm[1:], v_sm[1:], cache_k, cache_v,
                             za_sm[t_s:].reshape(n_s, t_s, D_ATTN), bias_c, bias_n, g_out.reshape(1, D_VHEAD))

    y_p = _finish(x_prompt.reshape(n_b * seq, D_MODEL), at_p.reshape(n_b * seq, D_ATTN), cvb_p, gc_p, ga_p,
                  wb0, wb1, wo, tm=512)
    y_s = _finish(x_sample.reshape(n_s * t_s, D_MODEL), at_s.reshape(n_s * t_s, D_ATTN), cvb_sm[t_s:],
                  gc_sm[t_s:], ga_sm[t_s:], wb0, wb1, wo, tm=n_s * t_s)

    return (y_p.reshape(n_b, seq, D_MODEL), y_s.reshape(n_s, t_s, D_MODEL), k_p[None], v_p[None],
            tail_p[None], k_sm[1:][None], v_sm[1:][None], tail_sm[1:][None])
```

```python
import functools
import math

import jax
import jax.numpy as jnp
import numpy as np
from jax import lax
from jax.experimental import pallas as pl
from jax.experimental.pallas import tpu as pltpu

D_MODEL = 2048
N_HEADS = 8
D_HEAD = 64
D_VHEAD = 2 * D_HEAD
D_ATTN = N_HEADS * D_VHEAD
C_CONV = D_MODEL // 2
CONV_WIDTH = 31
HALO = CONV_WIDTH - 1
N_META = 16
CHUNK = 64
N_BUCKETS = 32
MAX_DISTANCE = 128
EPS = 1e-6
NEG = -1e30
SCALE = D_HEAD ** -0.5
LOG2E = math.log2(math.e)
Q_SCALE = SCALE * LOG2E

COL_TILE = 512
N_COL_STEPS = 22
STEP_GLU, STEP_ZC, STEP_Q, STEP_K, STEP_V, STEP_ZA, STEP_GC, STEP_GA, STEP_END = 0, 4, 6, 8, 10, 12, 14, 18, 22
N_W_RESIDENT = 6
N_W_SLOTS = 4
assert N_W_RESIDENT >= 1 and (N_COL_STEPS - N_W_RESIDENT - N_W_SLOTS) % 2 == 0
HALO_PAD = 32
CONV_ROWS = 32
ATT_TILE = 256
VMEM_LIMIT = 56 * 1024 * 1024
VMEM_LIMIT_PROJ = 60 * 1024 * 1024

_BF16 = jnp.bfloat16
_F32 = jnp.float32


def _sigmoid(x):
    return 1.0 / (1.0 + jnp.exp(-x))


def _silu(x):
    return x * _sigmoid(x)


def _head_norm(u, gain_ref, h0, scale):
    lane = lax.broadcasted_iota(jnp.int32, (1, D_VHEAD), 1)
    lo = lane < D_HEAD
    outs = []
    for hl in range(u.shape[1] // D_VHEAD):
        xh = u[:, hl * D_VHEAD:(hl + 1) * D_VHEAD]
        sq = xh * xh
        s_lo = jnp.sum(jnp.where(lo, sq, 0.0), axis=-1, keepdims=True)
        s_hi = jnp.sum(jnp.where(lo, 0.0, sq), axis=-1, keepdims=True)
        r = jnp.where(lo, lax.rsqrt(s_lo * (1.0 / D_HEAD) + EPS), lax.rsqrt(s_hi * (1.0 / D_HEAD) + EPS))
        y = xh * r * gain_ref[h0 + hl:h0 + hl + 1, :]
        if scale != 1.0:
            y = y * scale
        outs.append(y)
    return outs


def _proj_kernel(x_ref, halo_ref, ng_ref, w_hbm, qg_ref, kg_ref, cw_ref, cb_ref, lng_ref, lnb_ref, *rest,
                 ns, tt, tiles_per_seq, kv_row_offset):
    kv_manual = kv_row_offset > 0
    if kv_manual:
        (km_ref, vm_ref, q_ref, k_hbm, v_hbm, za_ref, cvb_ref, gc_ref, ga_ref, tail_ref,
         h_ref, *w_buf, u_ref0, u_ref1, g_ref, zc_ref, y_ref, sem, k_stage, v_stage, kv_sem) = rest
    else:
        (q_ref, k_ref, v_ref, za_ref, cvb_ref, gc_ref, ga_ref, tail_ref,
         h_ref, *w_buf, u_ref0, u_ref1, g_ref, zc_ref, y_ref, sem) = rest
    assert len(w_buf) == N_W_RESIDENT + N_W_SLOTS
    i = pl.program_id(0)
    n_i = pl.num_programs(0)
    seq_rows = HALO_PAD + tt
    hpt = COL_TILE // D_VHEAD
    lpt = COL_TILE // 128
    n_lane_tiles = C_CONV // 128
    rb = min(tt, CONV_ROWS)

    u_ref = (u_ref0, u_ref1)

    def w_index(step):
        return step if step < N_W_RESIDENT else N_W_RESIDENT + (step - N_W_RESIDENT) % N_W_SLOTS

    def w_copy(step):
        return pltpu.make_async_copy(w_hbm.at[step], w_buf[w_index(step)], sem.at[w_index(step)])

    @pl.when(i == 0)
    def _():
        for t in range(N_W_RESIDENT + N_W_SLOTS):
            w_copy(t).start()
        for t in range(N_W_RESIDENT):
            w_copy(t).wait()

    if kv_manual:
        assert ns == 1
        seq_idx = i // tiles_per_seq
        row0 = pl.multiple_of(kv_row_offset + (i % tiles_per_seq) * tt, 8)

        def kv_copy(which):
            if which < 2:
                src, dst = (k_stage, k_hbm) if which == 0 else (v_stage, v_hbm)
                dst = dst.at[seq_idx, :, pl.ds(row0, tt), :]
            else:
                src, dst = (km_ref, k_hbm) if which == 2 else (vm_ref, v_hbm)
                dst = dst.at[seq_idx, :, pl.ds(0, kv_row_offset), :]
            return pltpu.make_async_copy(src, dst, kv_sem.at[which])

        @pl.when(i % tiles_per_seq == 0)
        def _():
            kv_copy(2).start()
            kv_copy(3).start()

    if tiles_per_seq == 1:
        for s in range(ns):
            for c in range(n_lane_tiles):
                g_ref[c, s * seq_rows + 2:s * seq_rows + HALO_PAD, :] = halo_ref[s, :, c * 128:(c + 1) * 128]
    else:
        @pl.when(i % tiles_per_seq == 0)
        def _():
            for c in range(n_lane_tiles):
                g_ref[c, 2:HALO_PAD, :] = halo_ref[0, :, c * 128:(c + 1) * 128]

    def form_h(r0, r1):
        x = x_ref[r0:r1, :]
        ms = jnp.mean(x * x, axis=-1, keepdims=True)
        h_ref[r0:r1, :] = (x * lax.rsqrt(ms + EPS) * ng_ref[...]).astype(_BF16)

    def split(n, parts, align):
        cuts = [align * round(n * k / (parts * align)) for k in range(parts + 1)]
        cuts[0], cuts[-1] = 0, n
        return list(zip(cuts[:-1], cuts[1:]))

    tm = ns * tt
    mm_pieces = [(rs, cs) for rs in split(tm, 2, 16) for cs in split(COL_TILE, 2, 256)]
    row_quarters = split(tm, 8, max(tt if ns > 1 else 16, 16))

    def matmul_piece(step, piece):
        (r0, r1), (c0, c1) = piece
        u_ref[step % 2][r0:r1, c0:c1] = jnp.dot(h_ref[r0:r1, :], w_buf[w_index(step)][:, c0:c1],
                                                preferred_element_type=_F32)

    def seq_pieces(r0, r1):
        out = []
        for s in range(ns):
            lo, hi = max(r0, s * tt), min(r1, (s + 1) * tt)
            if lo < hi:
                out.append((s, lo - s * tt, hi - s * tt))
        return out

    def epilogue(step, r0, r1):
        if r0 == r1:
            return
        u = u_ref[step % 2][r0:r1, :]
        pieces = seq_pieces(r0, r1)

        def rows(s, lo, hi):
            return slice(s * tt + lo - r0, s * tt + hi - r0)

        if STEP_Q <= step < STEP_V:
            is_q = step < STEP_K
            h0 = (step - (STEP_Q if is_q else STEP_K)) * hpt
            ys = _head_norm(u, qg_ref if is_q else kg_ref, h0, Q_SCALE if is_q else 1.0)
            for hl in range(hpt):
                for (s, lo, hi) in pieces:
                    if is_q:
                        q_ref[s, h0 + hl, lo:hi, :] = ys[hl][rows(s, lo, hi)].astype(_BF16)
                    elif kv_manual:
                        k_stage[h0 + hl, lo:hi, :] = ys[hl][rows(s, lo, hi)]
                    else:
                        k_ref[s, h0 + hl, lo:hi, :] = ys[hl][rows(s, lo, hi)]
        elif STEP_V <= step < STEP_ZA:
            h0 = (step - STEP_V) * hpt
            for hl in range(hpt):
                for (s, lo, hi) in pieces:
                    blk = u[rows(s, lo, hi), hl * D_VHEAD:(hl + 1) * D_VHEAD]
                    if kv_manual:
                        v_stage[h0 + hl, lo:hi, :] = blk
                    else:
                        v_ref[s, h0 + hl, lo:hi, :] = blk
        elif STEP_ZA <= step < STEP_GC:
            c0 = (step - STEP_ZA) * COL_TILE
            za_ref[r0:r1, c0:c0 + COL_TILE] = _silu(u).astype(_BF16)
        elif STEP_GLU <= step < STEP_ZC:
            k = step - STEP_GLU
            lt0 = (k // 2) * lpt
            gate = _sigmoid(u) if k % 2 else None
            for (s, lo, hi) in pieces:
                dst = slice(s * seq_rows + HALO_PAD + lo, s * seq_rows + HALO_PAD + hi)
                for c in range(lpt):
                    blk = u[rows(s, lo, hi), c * 128:(c + 1) * 128]
                    if gate is None:
                        g_ref[lt0 + c, dst, :] = blk
                    else:
                        g_ref[lt0 + c, dst, :] = g_ref[lt0 + c, dst, :] * gate[rows(s, lo, hi), c * 128:(c + 1) * 128]
        elif STEP_ZC <= step < STEP_Q:
            c0 = (step - STEP_ZC) * COL_TILE
            zc_ref[r0:r1, c0:c0 + COL_TILE] = _silu(u)
        elif STEP_GC <= step < STEP_GA:
            c0 = (step - STEP_GC) * COL_TILE
            gc_ref[r0:r1, c0:c0 + COL_TILE] = _sigmoid(u).astype(_BF16)
        else:
            assert STEP_GA <= step < STEP_END
            c0 = (step - STEP_GA) * COL_TILE
            ga_ref[r0:r1, c0:c0 + COL_TILE] = _sigmoid(u).astype(_BF16)

    def conv_unit(s, r0, c):
        cs = slice(c * 128, (c + 1) * 128)
        base = s * seq_rows + r0
        acc = jnp.broadcast_to(cb_ref[:, cs], (rb, 128))
        for r in range(8):
            taps = [(a, 8 * a + r - 2) for a in range(HALO_PAD // 8 + 1) if 0 <= 8 * a + r - 2 < CONV_WIDTH]
            phase = g_ref[c, base + r:base + r + 8 * taps[-1][0] + rb, :]
            for a, t in taps:
                acc = acc + cw_ref[t:t + 1, cs] * phase[8 * a:8 * a + rb]
        y_ref[s * tt + r0:s * tt + r0 + rb, cs] = acc

    def norm_unit(s, r0):
        rows = slice(s * tt + r0, s * tt + r0 + rb)
        ys = [y_ref[rows, c * 128:(c + 1) * 128] for c in range(n_lane_tiles)]
        tot = ys[0]
        for c in range(1, n_lane_tiles):
            tot = tot + ys[c]
        mu = jnp.sum(tot, axis=-1, keepdims=True) * (1.0 / C_CONV)
        cen = [a - mu for a in ys]
        sq = cen[0] * cen[0]
        for c in range(1, n_lane_tiles):
            sq = sq + cen[c] * cen[c]
        rstd = lax.rsqrt(jnp.sum(sq, axis=-1, keepdims=True) * (1.0 / C_CONV) + EPS)
        for c in range(n_lane_tiles):
            cs = slice(c * 128, (c + 1) * 128)
            y = _silu(cen[c] * rstd * lng_ref[:, cs] + lnb_ref[:, cs])
            cvb_ref[rows, cs] = (y * zc_ref[rows, cs]).astype(_BF16)

    blocks = [(s, r0) for s in range(ns) for r0 in range(0, tt, rb)]
    vregs = rb // 8
    item_cost = {"conv": (2 * CONV_WIDTH + 2) * vregs, "norm": 18 * vregs * n_lane_tiles}
    norm_lag = 2

    def epilogue_cost(step):
        per_vreg = (11 if STEP_Q <= step < STEP_V else 0 if STEP_V <= step < STEP_ZA
                    else 5 if step % 2 or step >= STEP_ZC else 0)
        return per_vreg * tm * COL_TILE // 1024

    items = []
    for bi in range(len(blocks) + norm_lag):
        if bi < len(blocks):
            items += [("conv",) + blocks[bi] + (c,) for c in range(n_lane_tiles)]
        if bi >= norm_lag:
            items.append(("norm",) + blocks[bi - norm_lag])
    work_steps = list(range(STEP_ZC, N_COL_STEPS - 1))
    target = (sum(item_cost[it[0]] for it in items) + sum(map(epilogue_cost, work_steps))) / len(work_steps)
    work_at, pos = {}, 0
    for st in work_steps:
        budget, first = target - epilogue_cost(st), pos
        while pos < len(items) and (budget > 0 or st == work_steps[-1]):
            if items[pos][0] == "norm" and st < STEP_Q:
                break
            budget -= item_cost[items[pos][0]]
            pos += 1
        work_at[st] = items[first:pos]
    assert pos == len(items)

    done_rows = 0
    for piece in mm_pieces:
        if piece[0][1] > done_rows:
            form_h(done_rows, piece[0][1])
            done_rows = piece[0][1]
        matmul_piece(0, piece)
    n_q = len(row_quarters)

    def vpu_chunk(step, k):
        work = work_at.get(step, [])
        epilogue(step, *row_quarters[k])
        for it in work[k * len(work) // n_q:(k + 1) * len(work) // n_q]:
            if it[0] == "conv":
                conv_unit(*it[1:])
            else:
                norm_unit(*it[1:])
        if kv_manual and k == n_q - 1 and step in (STEP_V - 1, STEP_ZA - 1):
            kv_copy(0 if step == STEP_V - 1 else 1).start()

    carried = None
    for step in range(N_COL_STEPS):
        if step == N_W_RESIDENT - 1:
            w_copy(step + 1).wait()
            w_copy(step + 2).wait()
        elif step > N_W_RESIDENT and (step - N_W_RESIDENT) % 2 == 1:
            for t in (step + 1, step + 2):
                if t < N_COL_STEPS:
                    w_copy(t).wait()
            for t in (step + 3, step + 4):
                if t < N_COL_STEPS:
                    w_copy(t).start()
        if carried is not None:
            vpu_chunk(*carried)
        per_piece = n_q // len(mm_pieces)
        for p, piece in enumerate(mm_pieces):
            if step + 1 < N_COL_STEPS:
                matmul_piece(step + 1, piece)
            for k in range(p * per_piece, (p + 1) * per_piece):
                if step + 1 < N_COL_STEPS and k == n_q - 1:
                    carried = (step, k)
                else:
                    vpu_chunk(step, k)

    if tiles_per_seq == 1:
        for s in range(ns):
            for c in range(n_lane_tiles):
                tail_ref[s, :, c * 128:(c + 1) * 128] = g_ref[c, (s + 1) * seq_rows - HALO:(s + 1) * seq_rows, :]
    else:
        @pl.when(i % tiles_per_seq == tiles_per_seq - 1)
        def _():
            for c in range(n_lane_tiles):
                tail_ref[0, :, c * 128:(c + 1) * 128] = g_ref[c, seq_rows - HALO:seq_rows, :]
        g_ref[:, 0:HALO_PAD, :] = g_ref[:, tt:tt + HALO_PAD, :]

    if kv_manual:
        kv_copy(0).wait()
        kv_copy(1).wait()

        @pl.when(i % tiles_per_seq == 0)
        def _():
            kv_copy(2).wait()
            kv_copy(3).wait()

    @pl.when(i + 1 < n_i)
    def _():
        for t in range(N_W_RESIDENT, N_W_RESIDENT + N_W_SLOTS):
            w_copy(t).start()


def _project(x, halo, ng, w, qg, kg, cw, cb, lng, lnb, *, n_seq, seq_len, tm, kv_front=None):
    kv_row_offset = kv_front[0].shape[1] if kv_front is not None else 0
    rows = n_seq * seq_len
    if seq_len >= tm:
        assert seq_len % tm == 0
        ns, tt, tps = 1, tm, seq_len // tm
    else:
        assert tm % seq_len == 0 and rows % tm == 0
        ns, tt, tps = tm // seq_len, seq_len, 1
    n_tiles = rows // tm
    halo_bcast = halo.shape[0] == 1

    def seq_map(i):
        return (i // tps, 0, i % tps, 0) if tps > 1 else (i, 0, 0, 0)

    def halo_map(i):
        if halo_bcast:
            return (0, 0, 0)
        return (i // tps, 0, 0) if tps > 1 else (i, 0, 0)

    def tail_map(i):
        return (i // tps, 0, 0) if tps > 1 else (i, 0, 0)

    const2 = lambda i: (0, 0)
    head_blk = (ns, N_HEADS, tt, D_VHEAD)
    if kv_row_offset:
        assert ns == 1
        kv_spec = pl.BlockSpec(memory_space=pl.ANY)
    else:
        kv_spec = pl.BlockSpec(head_blk, seq_map)
    kv_rows = kv_row_offset + seq_len
    row = lambda i: (i, 0)
    kern = functools.partial(_proj_kernel, ns=ns, tt=tt, tiles_per_seq=tps, kv_row_offset=kv_row_offset)
    out_shape = (
        jax.ShapeDtypeStruct((n_seq, N_HEADS, seq_len, D_VHEAD), _BF16),
        jax.ShapeDtypeStruct((n_seq, N_HEADS, kv_rows, D_VHEAD), _F32),
        jax.ShapeDtypeStruct((n_seq, N_HEADS, kv_rows, D_VHEAD), _F32),
        jax.ShapeDtypeStruct((rows, D_ATTN), _BF16),
        jax.ShapeDtypeStruct((rows, C_CONV), _BF16),
        jax.ShapeDtypeStruct((rows, D_MODEL), _BF16),
        jax.ShapeDtypeStruct((rows, D_MODEL), _BF16),
        jax.ShapeDtypeStruct((n_seq, HALO, C_CONV), _F32),
    )
    in_specs = [
        pl.BlockSpec((tm, D_MODEL), row),
        pl.BlockSpec((1 if halo_bcast or tps > 1 else ns, HALO, C_CONV), halo_map),
        pl.BlockSpec((1, D_MODEL), const2),
        pl.BlockSpec(memory_space=pl.ANY),
        pl.BlockSpec((N_HEADS, D_VHEAD), const2),
        pl.BlockSpec((N_HEADS, D_VHEAD), const2),
        pl.BlockSpec((CONV_WIDTH, C_CONV), const2),
        pl.BlockSpec((1, C_CONV), const2),
        pl.BlockSpec((1, C_CONV), const2),
        pl.BlockSpec((1, C_CONV), const2),
    ]
    operands = [x, halo, ng, w, qg, kg, cw, cb, lng, lnb]
    if kv_row_offset:
        front = pl.BlockSpec((N_HEADS, kv_row_offset, D_VHEAD), lambda i: (0, 0, 0))
        in_specs += [front, front]
        operands += list(kv_front)
    out_specs = (
        pl.BlockSpec(head_blk, seq_map),
        kv_spec,
        kv_spec,
        pl.BlockSpec((tm, D_ATTN), row),
        pl.BlockSpec((tm, C_CONV), row),
        pl.BlockSpec((tm, D_MODEL), row),
        pl.BlockSpec((tm, D_MODEL), row),
        pl.BlockSpec((1 if tps > 1 else ns, HALO, C_CONV), tail_map),
    )
    scratch = [
        pltpu.VMEM((tm, D_MODEL), _BF16),
    ] + [pltpu.VMEM((D_MODEL, COL_TILE), _BF16) for _ in range(N_W_RESIDENT + N_W_SLOTS)] + [
        pltpu.VMEM((tm, COL_TILE), _F32),
        pltpu.VMEM((tm, COL_TILE), _F32),
        pltpu.VMEM((C_CONV // 128, ns * (HALO_PAD + tt), 128), _F32),
        pltpu.VMEM((tm, C_CONV), _F32),
        pltpu.VMEM((tm, C_CONV), _F32),
        pltpu.SemaphoreType.DMA((N_W_RESIDENT + N_W_SLOTS,)),
    ]
    if kv_row_offset:
        scratch += [
            pltpu.VMEM((N_HEADS, tt, D_VHEAD), _F32),
            pltpu.VMEM((N_HEADS, tt, D_VHEAD), _F32),
            pltpu.SemaphoreType.DMA((4,)),
        ]
    return pl.pallas_call(
        kern, grid=(n_tiles,), in_specs=in_specs, out_specs=out_specs, out_shape=out_shape,
        scratch_shapes=scratch,
        compiler_params=pltpu.CompilerParams(
            dimension_semantics=("arbitrary",), vmem_limit_bytes=VMEM_LIMIT_PROJ),
        name="proj",
    )(*operands)


def _attn_kernel(lam_ref, q_ref, k_ref, v_ref, za_ref, bd0_ref, bd1_ref, bm_ref, g_ref,
                 o_ref, kb_ref, vt_ref, s_ref, p_ref, *, seq_len):
    T = ATT_TILE
    n_q = seq_len // T
    lam = lam_ref[0]
    kb_ref[...] = k_ref[0, 0, N_META:, :].astype(_BF16)
    vt_ref[...] = v_ref[0, 0, N_META:, :].T.astype(_BF16)
    kmb = k_ref[0, 0, 0:N_META, :].astype(_BF16)
    vmt = v_ref[0, 0, 0:N_META, :].T.astype(_BF16)
    lane = lax.broadcasted_iota(jnp.int32, (1, D_VHEAD), 1)
    lo = lane < D_HEAD
    nt = (((1,), (1,)), ((), ()))

    def col_reduce(x, op):
        return op(x.reshape(x.shape[0] // 8, 8, T), axis=0)

    items = [(i, mp) for i in range(n_q) for mp in range(2)]
    qs_cache, m8, sm, mx, l8, outs = {}, {}, {}, {}, {}, {}

    def q_of(w):
        i, mp = items[w]
        if i not in qs_cache:
            q = q_ref[0, 0, i * T:(i + 1) * T, :]
            zero = jnp.zeros_like(q)
            qs_cache[i] = (jnp.where(lo, q, zero), jnp.where(lo, zero, q))
        return qs_cache[i][mp]

    def score_tile(w, j):
        i, _ = items[w]
        s = lax.dot_general(kb_ref[j * T:(j + 1) * T, :], q_of(w), nt, preferred_element_type=_F32)
        if j == i:
            s = s + bd0_ref[0]
        elif j == i - 1:
            s = s + bd1_ref[0]
        s_ref[w % 2, j * T:(j + 1) * T, :] = s
        t8 = col_reduce(s, jnp.max)
        m8[w] = t8 if j == 0 else jnp.maximum(m8[w], t8)

    def score_finish(w):
        i, _ = items[w]
        s = lax.dot_general(kmb, q_of(w), nt, preferred_element_type=_F32)
        if i == 0:
            s = s + bm_ref[0]
        sm[w] = s
        mx[w] = jnp.maximum(jnp.max(m8[w], axis=0, keepdims=True), jnp.max(s, axis=0, keepdims=True))

    def exp_tile(w, j):
        p = jnp.exp2(s_ref[w % 2, j * T:(j + 1) * T, :] - mx[w])
        p_ref[w % 2, j * T:(j + 1) * T, :] = p.astype(_BF16)
        t8 = col_reduce(p, jnp.sum)
        l8[w] = t8 if j == 0 else l8[w] + t8

    def value_matmul(w):
        i, _ = items[w]
        nk = (i + 1) * T
        pm = jnp.exp2(sm[w] - mx[w])
        l = jnp.sum(l8[w], axis=0, keepdims=True) + jnp.sum(pm, axis=0, keepdims=True)
        acc = (jnp.dot(vt_ref[:, 0:nk], p_ref[w % 2, 0:nk, :], preferred_element_type=_F32)
               + jnp.dot(vmt, pm.astype(_BF16), preferred_element_type=_F32))
        outs[w] = acc / l

    def finalize(i):
        o = outs[2 * i] - lam * outs[2 * i + 1]
        ms = jnp.mean(o * o, axis=0, keepdims=True)
        y = o * lax.rsqrt(ms + EPS) * g_ref[...]
        yt = y.T
        o_ref[0, i * T:(i + 1) * T, :] = (yt * za_ref[0, i * T:(i + 1) * T, :].astype(_F32)).astype(_BF16)

    n_items = len(items)
    for j in range(items[0][0] + 1):
        score_tile(0, j)
    score_finish(0)
    for w in range(n_items):
        n_exp = items[w][0] + 1
        n_next = items[w + 1][0] + 1 if w + 1 < n_items else 0
        for j in range(max(n_exp, n_next)):
            if j < n_next:
                score_tile(w + 1, j)
            if j < n_exp:
                exp_tile(w, j)
        if n_next:
            score_finish(w + 1)
        value_matmul(w)
        if items[w][1] == 1:
            finalize(items[w][0])


def _attention(lam, q, k, v, za, bd0, bd1, bm, g):
    n_b, _, seq_len, _ = q.shape
    T = ATT_TILE
    assert seq_len % T == 0
    kern = functools.partial(_attn_kernel, seq_len=seq_len)
    bh = lambda b, h: (b, h, 0, 0)
    hd = lambda b, h: (h, 0, 0)
    in_specs = [
        pl.BlockSpec(memory_space=pltpu.SMEM),
        pl.BlockSpec((1, 1, seq_len, D_VHEAD), bh),
        pl.BlockSpec((1, 1, N_META + seq_len, D_VHEAD), bh),
        pl.BlockSpec((1, 1, N_META + seq_len, D_VHEAD), bh),
        pl.BlockSpec((1, seq_len, D_VHEAD), lambda b, h: (b, 0, h)),
        pl.BlockSpec((1, T, T), hd),
        pl.BlockSpec((1, T, T), hd),
        pl.BlockSpec((1, N_META, T), hd),
        pl.BlockSpec((D_VHEAD, T), lambda b, h: (0, 0)),
    ]
    return pl.pallas_call(
        kern, grid=(n_b, N_HEADS), in_specs=in_specs,
        out_specs=pl.BlockSpec((1, seq_len, D_VHEAD), lambda b, h: (b, 0, h)),
        out_shape=jax.ShapeDtypeStruct((n_b, seq_len, D_ATTN), _BF16),
        scratch_shapes=[
            pltpu.VMEM((seq_len, D_VHEAD), _BF16),
            pltpu.VMEM((D_VHEAD, seq_len), _BF16),
            pltpu.VMEM((2, seq_len, T), _F32),
            pltpu.VMEM((2, seq_len, T), _BF16),
        ],
        compiler_params=pltpu.CompilerParams(
            dimension_semantics=("arbitrary", "arbitrary"), vmem_limit_bytes=VMEM_LIMIT),
        name="attn",
    )(lam, q, k, v, za, bd0, bd1, bm, g)


def _decode_kernel(lam_ref, q_ref, kn_ref, vn_ref, ck_ref, cv_ref, za_ref, bc_ref, bn_ref, g_ref, o_ref):
    lam = lam_ref[0]
    lane = lax.broadcasted_iota(jnp.int32, (1, D_VHEAD), 1)
    lo = lane < D_HEAD
    nt = (((1,), (1,)), ((), ()))
    scores, weights = {}, {}

    def score_stage(h):
        q = q_ref[0, h]
        zero = jnp.zeros_like(q)
        qs = (jnp.where(lo, q, zero), jnp.where(lo, zero, q))
        ck = ck_ref[0, 0, h].astype(_BF16)
        kn = kn_ref[0, h].astype(_BF16)
        scores[h] = [(lax.dot_general(qs[mp], ck, nt, preferred_element_type=_F32) + bc_ref[h],
                      lax.dot_general(qs[mp], kn, nt, preferred_element_type=_F32) + bn_ref[h])
                     for mp in range(2)]

    def softmax_stage(h):
        ps = []
        for sc, sn in scores.pop(h):
            m = jnp.maximum(jnp.max(sc, axis=-1, keepdims=True), jnp.max(sn, axis=-1, keepdims=True))
            ec = jnp.exp2(sc - m)
            en = jnp.exp2(sn - m)
            l = jnp.sum(ec, axis=-1, keepdims=True) + jnp.sum(en, axis=-1, keepdims=True)
            ps.append((ec / l, en / l))
        weights[h] = ((ps[0][0] - lam * ps[1][0]).astype(_BF16), (ps[0][1] - lam * ps[1][1]).astype(_BF16))

    def value_stage(h):
        ac, an = weights.pop(h)
        o = (jnp.dot(ac, cv_ref[0, 0, h].astype(_BF16), preferred_element_type=_F32)
             + jnp.dot(an, vn_ref[0, h].astype(_BF16), preferred_element_type=_F32))
        ms = jnp.mean(o * o, axis=-1, keepdims=True)
        y = o * lax.rsqrt(ms + EPS) * g_ref[...]
        cs = slice(h * D_VHEAD, (h + 1) * D_VHEAD)
        o_ref[0, :, cs] = (y * za_ref[0, :, cs].astype(_F32)).astype(_BF16)

    for t in range(N_HEADS + 2):
        if t < N_HEADS:
            score_stage(t)
        if 0 <= t - 1 < N_HEADS:
            softmax_stage(t - 1)
        if 0 <= t - 2 < N_HEADS:
            value_stage(t - 2)


def _decode_attention(lam, q, kn, vn, ck, cv, za, bc, bn, g):
    n_b, _, t_s, _ = q.shape
    past = ck.shape[3]
    b4 = lambda b: (b, 0, 0, 0)
    c3 = lambda b: (0, 0, 0)
    in_specs = [
        pl.BlockSpec(memory_space=pltpu.SMEM),
        pl.BlockSpec((1, N_HEADS, t_s, D_VHEAD), b4),
        pl.BlockSpec((1, N_HEADS, t_s, D_VHEAD), b4),
        pl.BlockSpec((1, N_HEADS, t_s, D_VHEAD), b4),
        pl.BlockSpec((1, 1, N_HEADS, past, D_VHEAD), lambda b: (0, b, 0, 0, 0)),
        pl.BlockSpec((1, 1, N_HEADS, past, D_VHEAD), lambda b: (0, b, 0, 0, 0)),
        pl.BlockSpec((1, t_s, D_ATTN), lambda b: (b, 0, 0)),
        pl.BlockSpec((N_HEADS, t_s, past), c3),
        pl.BlockSpec((N_HEADS, t_s, t_s), c3),
        pl.BlockSpec((1, D_VHEAD), lambda b: (0, 0)),
    ]
    return pl.pallas_call(
        _decode_kernel, grid=(n_b,), in_specs=in_specs,
        out_specs=pl.BlockSpec((1, t_s, D_ATTN), lambda b: (b, 0, 0)),
        out_shape=jax.ShapeDtypeStruct((n_b, t_s, D_ATTN), _BF16),
        compiler_params=pltpu.CompilerParams(
            dimension_semantics=("arbitrary",), vmem_limit_bytes=VMEM_LIMIT),
        name="decode_attn",
    )(lam, q, kn, vn, ck, cv, za, bc, bn, g)


def _finish_kernel(x_ref, at_ref, cv_ref, gc_ref, ga_ref, wb0_ref, wb1_ref, wo_ref, y_ref):
    pc = jnp.dot(cv_ref[...], wb0_ref[...], preferred_element_type=_F32)
    pa = jnp.dot(at_ref[...], wb1_ref[...], preferred_element_type=_F32)
    merged = gc_ref[...].astype(_F32) * pc + ga_ref[...].astype(_F32) * pa
    y_ref[...] = x_ref[...] + jnp.dot(merged.astype(_BF16), wo_ref[...], preferred_element_type=_F32)


def _finish(x, at, cv, gc, ga, wb0, wb1, wo, *, tm):
    rows = x.shape[0]
    assert rows % tm == 0
    row = lambda i: (i, 0)
    const = lambda i: (0, 0)
    single = pl.Buffered(1)
    in_specs = [
        pl.BlockSpec((tm, D_MODEL), row),
        pl.BlockSpec((tm, D_ATTN), row),
        pl.BlockSpec((tm, C_CONV), row),
        pl.BlockSpec((tm, D_MODEL), row),
        pl.BlockSpec((tm, D_MODEL), row),
        pl.BlockSpec((C_CONV, D_MODEL), const, pipeline_mode=single),
        pl.BlockSpec((D_ATTN, D_MODEL), const, pipeline_mode=single),
        pl.BlockSpec((D_MODEL, D_MODEL), const, pipeline_mode=single),
    ]
    return pl.pallas_call(
        _finish_kernel, grid=(rows // tm,), in_specs=in_specs,
        out_specs=pl.BlockSpec((tm, D_MODEL), row),
        out_shape=jax.ShapeDtypeStruct((rows, D_MODEL), _F32),
        compiler_params=pltpu.CompilerParams(
            dimension_semantics=("arbitrary",), vmem_limit_bytes=VMEM_LIMIT),
        name="finish",
    )(x, at, cv, gc, ga, wb0, wb1, wo)


def _bucket_bias(rel, table):
    half = N_BUCKETS // 2
    max_exact = half // 2
    n = jnp.abs(rel)
    n_f = jnp.maximum(n, 1).astype(jnp.float32)
    large = max_exact + (jnp.log(n_f / max_exact) / math.log(MAX_DISTANCE / max_exact)
                         * (half - max_exact)).astype(jnp.int32)
    large = jnp.minimum(large, half - 1)
    bucket = jnp.where(rel > 0, half, 0) + jnp.where(n < max_exact, n, large)
    onehot = (bucket[..., None] == jnp.arange(N_BUCKETS)).astype(jnp.float32)
    return jnp.einsum('...b,bh->...h', onehot, table.astype(jnp.float32), precision=lax.Precision.HIGHEST)


def _permute_w_in(w):
    qk_w = N_HEADS * D_HEAD
    d = w.shape[0]

    def heads(c0):
        return w[:, c0:c0 + 2 * qk_w].reshape(d, 2, N_HEADS, D_HEAD).transpose(0, 2, 1, 3).reshape(d, 2 * qk_w)

    c_ag = 4 * qk_w + 2 * D_ATTN
    half = COL_TILE
    ag = w[:, c_ag:c_ag + 2 * C_CONV].reshape(d, 2, C_CONV // half, half).transpose(0, 2, 1, 3).reshape(d, 2 * C_CONV)
    c_zc = c_ag + 2 * C_CONV
    parts = [ag, w[:, c_zc:c_zc + C_CONV], heads(0), heads(2 * qk_w), w[:, 4 * qk_w:c_ag], w[:, c_zc + C_CONV:]]
    w = jnp.concatenate(parts, axis=1).astype(_BF16)
    return w.reshape(d, N_COL_STEPS, COL_TILE).transpose(1, 0, 2)


def kernel(x_prompt, x_sample, cache_k, cache_v, state_conv, meta_tokens, rel_bias, norm_gain, w_in,
           q_norm_gain, k_norm_gain, lambda_qk, subln_gain, conv_w, conv_b, conv_ln_gain, conv_ln_bias,
           w_branch_out, w_out):
    n_b, seq, _ = x_prompt.shape
    n_s, t_s, _ = x_sample.shape
    past = cache_k.shape[3]
    assert w_in.shape[0] == 1 and t_s == N_META and seq % ATT_TILE == 0
    layer = 0
    lam_init = 0.8 - 0.6 * math.exp(-0.3 * layer)

    lq = lambda_qk[layer].astype(_F32)
    lam = (jnp.exp(jnp.sum(lq[0] * lq[1])) - jnp.exp(jnp.sum(lq[2] * lq[3])) + lam_init).reshape(1)

    w = _permute_w_in(w_in[layer])
    ng = norm_gain[layer].reshape(1, D_MODEL)
    qg = jnp.tile(q_norm_gain[layer], (1, 2))
    kg = jnp.tile(k_norm_gain[layer], (1, 2))
    cw = conv_w[layer]
    cb = conv_b[layer].reshape(1, C_CONV)
    lng = conv_ln_gain[layer].reshape(1, C_CONV)
    lnb = conv_ln_bias[layer].reshape(1, C_CONV)
    wb0 = w_branch_out[layer, 0].astype(_BF16)
    wb1 = w_branch_out[layer, 1].astype(_BF16)
    wo = w_out[layer].astype(_BF16)
    g_out = subln_gain[layer].astype(_F32) * (1.0 - lam_init)
    proj = functools.partial(_project, ng=ng, w=w, qg=qg, kg=kg, cw=cw, cb=cb, lng=lng, lnb=lnb)

    x_small = jnp.concatenate([meta_tokens.astype(_F32), x_sample.reshape(n_s * t_s, D_MODEL)], axis=0)
    halo_small = jnp.concatenate([jnp.zeros((1, HALO, C_CONV), _F32), state_conv[layer].astype(_F32)], axis=0)
    q_sm, k_sm, v_sm, za_sm, cvb_sm, gc_sm, ga_sm, tail_sm = proj(
        x_small, halo_small, n_seq=n_s + 1, seq_len=t_s, tm=(n_s + 1) * t_s)

    q_p, k_p, v_p, za_p, cvb_p, gc_p, ga_p, tail_p = proj(
        x_prompt.reshape(n_b * seq, D_MODEL), tail_sm[0:1], n_seq=n_b, seq_len=seq, tm=512,
        kv_front=(k_sm[0], v_sm[0]))

    T = ATT_TILE
    far = rel_bias[N_BUCKETS // 2 - 1].astype(_F32)
    r = jnp.arange(T)
    rel_d = r[:, None] - r[None, :]
    allowed = (r[:, None] // CHUNK) <= (r[None, :] // CHUNK)
    bd0 = jnp.where(allowed[None], jnp.transpose(_bucket_bias(rel_d, rel_bias) - far, (2, 0, 1)) * LOG2E, NEG)
    bd1 = jnp.transpose(_bucket_bias(rel_d - T, rel_bias) - far, (2, 0, 1)) * LOG2E
    rel_m = jnp.arange(N_META)[:, None] - (N_META + r[None, :])
    bm = jnp.transpose(_bucket_bias(rel_m, rel_bias) - far, (2, 0, 1)) * LOG2E
    g_t = jnp.broadcast_to(g_out[:, None], (D_VHEAD, T))

    at_p = _attention(lam, q_p, k_p, v_p, za_p.reshape(n_b, seq, D_ATTN), bd0, bd1, bm, g_t)

    q_pos = past + jnp.arange(t_s)
    bias_c = jnp.transpose(_bucket_bias(jnp.arange(past)[None, :] - q_pos[:, None], rel_bias), (2, 0, 1)) * LOG2E
    bias_n = jnp.transpose(_bucket_bias(q_pos[None, :] - q_pos[:, None], rel_bias), (2, 0, 1)) * LOG2E
    at_s = _decode_attention(lam, q_sm[1:], k_sm[1:], v_sm[1:], cache_k, cache_v,
                             za_sm[t_s:].reshape(n_s, t_s, D_ATTN), bias_c, bias_n, g_out.reshape(1, D_VHEAD))

    y_p = _finish(x_prompt.reshape(n_b * seq, D_MODEL), at_p.reshape(n_b * seq, D_ATTN), cvb_p, gc_p, ga_p,
                  wb0, wb1, wo, tm=512)
    y_s = _finish(x_sample.reshape(n_s * t_s, D_MODEL), at_s.reshape(n_s * t_s, D_ATTN), cvb_sm[t_s:],
                  gc_sm[t_s:], ga_sm[t_s:], wb0, wb1, wo, tm=n_s * t_s)

    return (y_p.reshape(n_b, seq, D_MODEL), y_s.reshape(n_s, t_s, D_MODEL), k_p[None], v_p[None],
            tail_p[None], k_sm[1:][None], v_sm[1:][None], tail_sm[1:][None])
```
